```python
import math
import jax
import jax.numpy as jnp
from jax import lax
import numpy as np

D_MODEL = 1024
BATCH = 8
SEQ = 2048
DEPTH = 4

HEAD_DIM = 64
QBLK = 128
RMS_EPS = 1e-6
NEG = -1e30
FORCE = 1e30

A_HEADS = 4
A_QK = HEAD_DIM // 2
A_V = HEAD_DIM
B_HEADS = 4
B_KV = 2
B_WINDOW = 128
C_HEADS = 8
C_KV = 2
CMP_LEN = 32
CMP_STRIDE = 16
CMP_HID = 128
SLC_LEN = 64
SLC_TOPN = 8
SLC_QBLK = 64
C_WINDOW = 512

A_W = A_HEADS * A_V
B_W = B_HEADS * HEAD_DIM
C_W = C_HEADS * HEAD_DIM
D_MIX = A_W + B_W + C_W
KV_B = B_KV * HEAD_DIM
KV_C = C_KV * HEAD_DIM
N_GATES = 3 * C_HEADS
SPLIT_SIZES = (A_W, A_W, A_W, A_W, B_W, KV_B, KV_B, B_W, C_W, KV_C, KV_C, KV_C, KV_C, KV_C, KV_C, N_GATES, C_W)
IN_COLS = 4 * A_W + 2 * B_W + 2 * KV_B + 2 * C_W + 6 * KV_C + N_GATES

kernel_name = 'hybrid_diffattn_swa_nsa_parallel_heads'


def rmsnorm(x, w):
    xf = x.astype(jnp.float32)
    y = xf * lax.rsqrt(jnp.mean(xf * xf, axis=-1, keepdims=True) + RMS_EPS)
    return (y * w.astype(jnp.float32)).astype(x.dtype)


def alibi_slopes(n):
    return jnp.asarray(2.0 ** (-8.0 * np.arange(1, n + 1) / n), dtype=jnp.float32)


def diff_attention(q1, q2, k1, k2, v, lam, slopes):
    b, s, h, dq = q1.shape
    nb = s // QBLK
    scale = dq ** -0.5
    kpos = jnp.arange(s)

    def one_block(args):
        qi, q1b, q2b = args
        t = qi * QBLK + jnp.arange(QBLK)
        dist = t[:, None] - kpos[None, :]
        valid = dist >= 0
        bias = -slopes[:, None, None] * dist.astype(jnp.float32)

        def probs(qb, kk):
            sc = jnp.einsum('bqhd,bkhd->bhqk', qb, kk).astype(jnp.float32) * scale + bias
            return jax.nn.softmax(jnp.where(valid, sc, NEG), axis=-1)

        a = probs(q1b, k1) - lam * probs(q2b, k2)
        return jnp.einsum('bhqk,bkhd->bqhd', a.astype(v.dtype), v)

    q1s = q1.reshape(b, nb, QBLK, h, dq).swapaxes(0, 1)
    q2s = q2.reshape(b, nb, QBLK, h, dq).swapaxes(0, 1)
    out = lax.map(one_block, (jnp.arange(nb), q1s, q2s))
    return out.swapaxes(0, 1).reshape(b, s, h, v.shape[-1])


def banded_attention(q, k, v, window, slopes, sinks=None):
    b, s, g, r, d = q.shape
    nb = s // QBLK
    npv = -(-window // QBLK)
    nk = (npv + 1) * QBLK

    def band(x):
        xp = jnp.pad(x, ((0, 0), (npv * QBLK, 0), (0, 0), (0, 0))).reshape(b, nb + npv, QBLK, g, d)
        return jnp.concatenate([xp[:, o:o + nb] for o in range(npv + 1)], axis=2)

    kb = band(k)
    vb = band(v)
    qb = q.reshape(b, nb, QBLK, g, r, d)
    sc = jnp.einsum('bnqgrd,bnkgd->bngrqk', qb, kb).astype(jnp.float32) * d ** -0.5
    qi = np.arange(QBLK)[:, None]
    kj = np.arange(nk)[None, :]
    dist = qi + npv * QBLK - kj
    kpos = np.arange(nb)[:, None, None] * QBLK - npv * QBLK + kj[None]
    valid = (dist[None] >= 0) & (dist[None] < window) & (kpos >= 0)
    sc = sc - slopes[:, :, None, None] * jnp.asarray(dist, dtype=jnp.float32)
    sc = jnp.where(valid[:, None, None], sc, NEG)
    if sinks is None:
        p = jax.nn.softmax(sc, axis=-1)
    else:
        sink = jnp.broadcast_to(sinks.astype(jnp.float32)[:, :, None, None], (b, nb, g, r, QBLK, 1))
        p = jax.nn.softmax(jnp.concatenate([sc, sink], axis=-1), axis=-1)[..., :nk]
    o = jnp.einsum('bngrqk,bnkgd->bnqgrd', p.astype(v.dtype), vb)
    return o.reshape(b, s, g, r, d)


def overlap_matrix(n_cmp, n_slc):
    cs = np.arange(n_cmp)[:, None] * CMP_STRIDE
    ss = np.arange(n_slc)[None, :] * SLC_LEN
    ov = np.minimum(cs + CMP_LEN, ss + SLC_LEN) - np.maximum(cs, ss)
    return jnp.asarray(np.maximum(ov, 0) / CMP_LEN, dtype=jnp.float32)


def compress(x, pe, w1, w2):
    b, s, g, d = x.shape
    nch = s // CMP_STRIDE
    ratio = CMP_LEN // CMP_STRIDE
    ncmp = nch - ratio + 1
    ch = x.reshape(b, nch, CMP_STRIDE, g, d)
    blocks = jnp.concatenate([ch[:, o:o + ncmp] for o in range(ratio)], axis=2)
    blocks = blocks + pe[:, None, :]
    flat = blocks.transpose(0, 1, 3, 2, 4).reshape(b, ncmp, g, CMP_LEN * d)
    return jax.nn.silu(flat @ w1) @ w2


def nsa_attention(q, kc, vc, ks, vs, kw, vw, gates, slopes, pe_k, pe_v, wk1, wk2, wv1, wv2):
    b, s, g, r, d = q.shape
    scale = d ** -0.5
    kcmp = compress(kc, pe_k, wk1, wk2)
    vcmp = compress(vc, pe_v, wv1, wv2)
    ncmp = kcmp.shape[1]
    t = jnp.arange(s)
    cend = jnp.arange(ncmp) * CMP_STRIDE + CMP_LEN - 1
    dist_c = t[:, None] - cend[None, :]
    valid_c = dist_c >= 0
    sc = jnp.einsum('bsgrd,bcgd->bgrsc', q, kcmp).astype(jnp.float32) * scale
    sc = sc - slopes[:, :, None, None] * dist_c.astype(jnp.float32)
    p_cmp = jax.nn.softmax(jnp.where(valid_c, sc, NEG), axis=-1) * valid_c
    o_cmp = jnp.einsum('bgrsc,bcgd->bsgrd', p_cmp.astype(vcmp.dtype), vcmp)
    n_slc = s // SLC_LEN
    topn = min(SLC_TOPN, n_slc)
    imp = jnp.einsum('bgrsc,cj->bsgj', p_cmp, overlap_matrix(ncmp, n_slc))
    blk = jnp.arange(n_slc)[None, :]
    cur = (t // SLC_LEN)[:, None]
    causal = blk * SLC_LEN <= t[:, None]
    forced = (blk == 0) | (blk == cur) | (blk == cur - 1)
    score = jnp.where(causal[:, None, :], jnp.where(forced[:, None, :], FORCE, imp), NEG)
    _, idx = lax.top_k(score, topn)
    ksb = ks.reshape(b, n_slc, SLC_LEN, g, d).transpose(0, 3, 1, 2, 4)
    vsb = vs.reshape(b, n_slc, SLC_LEN, g, d).transpose(0, 3, 1, 2, 4)
    nq = s // SLC_QBLK
    bi = jnp.arange(b)[:, None, None, None]
    gi = jnp.arange(g)[None, None, :, None]

    def one_block(args):
        qi, qb, ib = args
        kg = ksb[bi, gi, ib].reshape(b, SLC_QBLK, g, topn * SLC_LEN, d)
        vg = vsb[bi, gi, ib].reshape(b, SLC_QBLK, g, topn * SLC_LEN, d)
        tq = qi * SLC_QBLK + jnp.arange(SLC_QBLK)
        kpos = (ib[..., None] * SLC_LEN + jnp.arange(SLC_LEN)).reshape(b, SLC_QBLK, g, topn * SLC_LEN)
        dist = (tq[None, :, None, None] - kpos)[:, :, :, None, :]
        scs = jnp.einsum('bqgrd,bqgkd->bqgrk', qb, kg).astype(jnp.float32) * scale
        scs = scs - slopes[None, None, :, :, None] * dist.astype(jnp.float32)
        p = jax.nn.softmax(jnp.where(dist >= 0, scs, NEG), axis=-1)
        return jnp.einsum('bqgrk,bqgkd->bqgrd', p.astype(vg.dtype), vg)

    qs = q.reshape(b, nq, SLC_QBLK, g, r, d).swapaxes(0, 1)
    ids = idx.reshape(b, nq, SLC_QBLK, g, topn).swapaxes(0, 1)
    o_slc = lax.map(one_block, (jnp.arange(nq), qs, ids)).swapaxes(0, 1).reshape(b, s, g, r, d)
    o_win = banded_attention(q, kw, vw, C_WINDOW, slopes)
    return gates[..., 0:1] * o_cmp + gates[..., 1:2] * o_slc + gates[..., 2:3] * o_win


def hybrid_layer(x, layer_idx, norm_w, w_in, w_out, lq1, lk1, lq2, lk2, subln_w, sinks,
                 pe_k, pe_v, wk1, wk2, wv1, wv2):
    b, s, _ = x.shape
    h = rmsnorm(x, norm_w) @ w_in
    offsets = np.cumsum(SPLIT_SIZES)[:-1].tolist()
    (qa, ka, va, ga, qb, kb, vb, gb, qc, kc, vc, ks, vs, kw, vw, gl, gc) = jnp.split(h, offsets, axis=-1)
    lam_init = 0.8 - 0.6 * math.exp(-0.3 * layer_idx)
    lam = (jnp.exp(jnp.sum(lq1.astype(jnp.float32) * lk1.astype(jnp.float32)))
           - jnp.exp(jnp.sum(lq2.astype(jnp.float32) * lk2.astype(jnp.float32))) + lam_init)
    qa = qa.reshape(b, s, A_HEADS, 2, A_QK)
    ka = ka.reshape(b, s, A_HEADS, 2, A_QK)
    oa = diff_attention(qa[..., 0, :], qa[..., 1, :], ka[..., 0, :], ka[..., 1, :],
                        va.reshape(b, s, A_HEADS, A_V), lam, alibi_slopes(A_HEADS))
    oa = rmsnorm(oa, subln_w) * (1.0 - lam_init)
    ya = oa.reshape(b, s, A_W) * jax.nn.silu(ga)
    rb = B_HEADS // B_KV
    ob = banded_attention(qb.reshape(b, s, B_KV, rb, HEAD_DIM), kb.reshape(b, s, B_KV, HEAD_DIM),
                          vb.reshape(b, s, B_KV, HEAD_DIM), B_WINDOW,
                          alibi_slopes(B_HEADS).reshape(B_KV, rb), sinks.reshape(B_KV, rb))
    yb = ob.reshape(b, s, B_W) * jax.nn.silu(gb)
    rc = C_HEADS // C_KV
    kvs = (b, s, C_KV, HEAD_DIM)
    gates = jax.nn.sigmoid(gl.reshape(b, s, C_KV, rc, 3))
    oc = nsa_attention(qc.reshape(b, s, C_KV, rc, HEAD_DIM), kc.reshape(kvs), vc.reshape(kvs),
                       ks.reshape(kvs), vs.reshape(kvs), kw.reshape(kvs), vw.reshape(kvs), gates,
                       alibi_slopes(C_HEADS).reshape(C_KV, rc), pe_k, pe_v, wk1, wk2, wv1, wv2)
    yc = oc.reshape(b, s, C_W) * jax.nn.silu(gc)
    return x + jnp.concatenate([ya, yb, yc], axis=-1) @ w_out


def setup_inputs(seed: int = 0) -> dict:
    key = jax.random.key(seed)
    ks = jax.random.split(key, 17)
    f32 = jnp.float32
    nrm = lambda k, shape, sc: jax.random.normal(k, shape, dtype=f32) * sc
    return {
        'x': nrm(ks[0], (BATCH, SEQ, D_MODEL), 1.0),
        'norm_w': 1.0 + nrm(ks[1], (DEPTH, D_MODEL), 0.02),
        'w_in': nrm(ks[2], (DEPTH, D_MODEL, IN_COLS), D_MODEL ** -0.5),
        'w_out': nrm(ks[3], (DEPTH, D_MIX, D_MODEL), D_MIX ** -0.5),
        'diff_lq1': nrm(ks[4], (DEPTH, A_QK), 0.1),
        'diff_lk1': nrm(ks[5], (DEPTH, A_QK), 0.1),
        'diff_lq2': nrm(ks[6], (DEPTH, A_QK), 0.1),
        'diff_lk2': nrm(ks[7], (DEPTH, A_QK), 0.1),
        'diff_subln': 1.0 + nrm(ks[8], (DEPTH, A_V), 0.02),
        'sinks': nrm(ks[9], (DEPTH, B_HEADS), 1.0),
        'cmp_pe_k': nrm(ks[10], (DEPTH, CMP_LEN, HEAD_DIM), 0.1),
        'cmp_pe_v': nrm(ks[11], (DEPTH, CMP_LEN, HEAD_DIM), 0.1),
        'cmp_wk1': nrm(ks[12], (DEPTH, CMP_LEN * HEAD_DIM, CMP_HID), (CMP_LEN * HEAD_DIM) ** -0.5),
        'cmp_wk2': nrm(ks[13], (DEPTH, CMP_HID, HEAD_DIM), CMP_HID ** -0.5),
        'cmp_wv1': nrm(ks[14], (DEPTH, CMP_LEN * HEAD_DIM, CMP_HID), (CMP_LEN * HEAD_DIM) ** -0.5),
        'cmp_wv2': nrm(ks[15], (DEPTH, CMP_HID, HEAD_DIM), CMP_HID ** -0.5),
        'final_norm': 1.0 + nrm(ks[16], (D_MODEL,), 0.02),
    }


def reference(x, norm_w, w_in, w_out, diff_lq1, diff_lk1, diff_lq2, diff_lk2, diff_subln, sinks,
              cmp_pe_k, cmp_pe_v, cmp_wk1, cmp_wk2, cmp_wv1, cmp_wv2, final_norm):
    for l in range(DEPTH):
        x = hybrid_layer(x, l, norm_w[l], w_in[l], w_out[l], diff_lq1[l], diff_lk1[l], diff_lq2[l],
                         diff_lk2[l], diff_subln[l], sinks[l], cmp_pe_k[l], cmp_pe_v[l],
                         cmp_wk1[l], cmp_wk2[l], cmp_wv1[l], cmp_wv2[l])
    return rmsnorm(x, final_norm)
```

```python
import functools
import math

import numpy as np
import jax
import jax.numpy as jnp
from jax import lax
from jax.experimental import pallas as pl
from jax.experimental.pallas import tpu as pltpu

F32 = jnp.float32
BF16 = jnp.bfloat16

HEAD_DIM = 64
RMS_EPS = 1e-6
NEG = -1e30
FORCE = 1e30

A_HEADS = 4
A_QK = HEAD_DIM // 2
B_HEADS = 4
B_KV = 2
B_WINDOW = 128
C_HEADS = 8
C_KV = 2
CMP_LEN = 32
CMP_STRIDE = 16
CMP_HID = 128
SLC_LEN = 64
SLC_TOPN = 8
C_WINDOW = 512

A_W = A_HEADS * HEAD_DIM
B_W = B_HEADS * HEAD_DIM
C_W = C_HEADS * HEAD_DIM
KV_B = B_KV * HEAD_DIM
KV_C = C_KV * HEAD_DIM
N_GATES = 3 * C_HEADS

LANES = 128
TQ = 128
TK = 256
TM = 512
VMEM_LIMIT = 48 * 1024 * 1024

_O = dict(qa=0, ka=256, va=512, ga=768, qb=1024, kb=1280, vb=1408, gb=1536, qc=1792,
          kc=2304, vc=2432, ks=2560, vs=2688, kw=2816, vw=2944, gl=3072, gc=3096)

Q_QC, Q_QA, Q_KA, Q_VA, Q_QB = 0, 512, 768, 1024, 1280
Q_KB, Q_VB, Q_KS, Q_VS, Q_KW, Q_VW = 1536, 1664, 1792, 1920, 2048, 2176
NQ = 2304
G_GC, G_GA, G_GB, G_GL, G_KC, G_VC = 0, 512, 768, 1024, 1152, 1280
NG = 1408
NTOT = NQ + NG


def _pair_perm(n_pairs):
    perm = np.zeros(n_pairs * 2 * HEAD_DIM, dtype=np.int64)
    for i in range(n_pairs):
        for g in range(2):
            head = g * n_pairs + i
            for d in range(HEAD_DIM):
                perm[i * 2 * HEAD_DIM + g * HEAD_DIM + d] = head * HEAD_DIM + d
    return perm


_PERM_B = _pair_perm(B_HEADS // B_KV)
_PERM_C = _pair_perm(C_HEADS // C_KV)


def _in_col_map():
    src = np.zeros(NTOT, dtype=np.int64)
    scale = np.zeros(NTOT, dtype=np.float32)

    def put(dst, cols, s=1.0):
        src[dst:dst + len(cols)] = cols
        scale[dst:dst + len(cols)] = s

    ar = np.arange
    put(Q_QC, _O['qc'] + _PERM_C, HEAD_DIM ** -0.5)
    put(Q_QA, _O['qa'] + ar(A_W), A_QK ** -0.5)
    put(Q_KA, _O['ka'] + ar(A_W))
    put(Q_VA, _O['va'] + ar(A_W))
    put(Q_QB, _O['qb'] + _PERM_B, HEAD_DIM ** -0.5)
    put(Q_KB, _O['kb'] + ar(KV_B))
    put(Q_VB, _O['vb'] + ar(KV_B))
    put(Q_KS, _O['ks'] + ar(KV_C))
    put(Q_VS, _O['vs'] + ar(KV_C))
    put(Q_KW, _O['kw'] + ar(KV_C))
    put(Q_VW, _O['vw'] + ar(KV_C))
    put(NQ + G_GC, _O['gc'] + _PERM_C)
    put(NQ + G_GA, _O['ga'] + ar(A_W))
    put(NQ + G_GB, _O['gb'] + _PERM_B)
    glc = np.zeros(N_GATES, dtype=np.int64)
    for br in range(3):
        for i in range(C_HEADS // C_KV):
            for g in range(C_KV):
                glc[br * C_HEADS + 2 * i + g] = g * 12 + i * 3 + br
    put(NQ + G_GL, _O['gl'] + glc)
    put(NQ + G_KC, _O['kc'] + ar(KV_C))
    put(NQ + G_VC, _O['vc'] + ar(KV_C))
    return src, scale


_IN_SRC, _IN_SCALE = _in_col_map()
_OUT_ROWS = np.concatenate([np.arange(A_W), A_W + _PERM_B, A_W + B_W + _PERM_C])


def _slopes(n):
    return [float(2.0 ** (-8.0 * h / n)) for h in range(1, n + 1)]


_SLOPE_A = _slopes(A_HEADS)
_SLOPE_B = _slopes(B_HEADS)
_SLOPE_C = _slopes(C_HEADS)


def _params(*sem):
    return pltpu.CompilerParams(dimension_semantics=sem, vmem_limit_bytes=VMEM_LIMIT)


def _dot_nt(a, b):
    return lax.dot_general(a, b, (((1,), (1,)), ((), ())), preferred_element_type=F32)


def _dot(a, b):
    return jnp.dot(a, b, preferred_element_type=F32)


def _silu(g):
    return g * (1.0 / (1.0 + jnp.exp(-g)))


def _left_half(shape):
    return lax.broadcasted_iota(jnp.int32, shape, len(shape) - 1) < HEAD_DIM


def _pair(left, right, rows):
    return jnp.where(_left_half((rows, LANES)), left, right)


def _in_proj_kernel(x_ref, nw_ref, w_ref, hq_ref, hg_ref):
    x = x_ref[...]
    ms = jnp.mean(x * x, axis=-1, keepdims=True)
    xn = ((x * lax.rsqrt(ms + RMS_EPS)) * nw_ref[...]).astype(BF16)
    step = 256
    for c0 in range(0, NQ, step):
        hq_ref[:, c0:c0 + step] = _dot(xn, w_ref[:, c0:c0 + step]).astype(BF16)
    for c0 in range(0, NG, LANES):
        hg_ref[:, c0:c0 + LANES] = _dot(xn, w_ref[:, NQ + c0:NQ + c0 + LANES])


def _in_proj(x2, nw, w):
    n, d = x2.shape
    return pl.pallas_call(
        _in_proj_kernel,
        grid=(n // TM,),
        in_specs=[pl.BlockSpec((TM, d), lambda i: (i, 0)),
                  pl.BlockSpec((1, d), lambda i: (0, 0)),
                  pl.BlockSpec((d, NTOT), lambda i: (0, 0))],
        out_specs=[pl.BlockSpec((TM, NQ), lambda i: (i, 0)),
                   pl.BlockSpec((TM, NG), lambda i: (i, 0))],
        out_shape=[jax.ShapeDtypeStruct((n, NQ), BF16), jax.ShapeDtypeStruct((n, NG), F32)],
        compiler_params=_params("arbitrary"),
        name="in_proj",
    )(x2, nw, w)


def _out_proj_kernel(ya_ref, yb_ref, yc_ref, w_ref, x_ref, fw_ref, o_ref, *, final):
    acc = x_ref[...]
    acc = acc + _dot(ya_ref[...], w_ref[0:A_W, :])
    acc = acc + _dot(yb_ref[...], w_ref[A_W:A_W + B_W, :])
    acc = acc + _dot(yc_ref[...], w_ref[A_W + B_W:, :])
    if final:
        ms = jnp.mean(acc * acc, axis=-1, keepdims=True)
        acc = (acc * lax.rsqrt(ms + RMS_EPS)) * fw_ref[...]
    o_ref[...] = acc


def _out_proj(ya, yb, yc, w, x2, fw, final):
    n, d = x2.shape
    return pl.pallas_call(
        functools.partial(_out_proj_kernel, final=final),
        grid=(n // TM,),
        in_specs=[pl.BlockSpec((TM, A_W), lambda i: (i, 0)),
                  pl.BlockSpec((TM, B_W), lambda i: (i, 0)),
                  pl.BlockSpec((TM, C_W), lambda i: (i, 0)),
                  pl.BlockSpec((A_W + B_W + C_W, d), lambda i: (0, 0)),
                  pl.BlockSpec((TM, d), lambda i: (i, 0)),
                  pl.BlockSpec((1, d), lambda i: (0, 0))],
        out_specs=pl.BlockSpec((TM, d), lambda i: (i, 0)),
        out_shape=jax.ShapeDtypeStruct((n, d), F32),
        compiler_params=_params("arbitrary"),
        name="out_proj",
    )(ya, yb, yc, w, x2, fw)


def _compress_kernel(kc_ref, vc_ref, pek_ref, pev_ref, wk1_ref, wk2_ref, wv1_ref, wv2_ref,
                     kcmp_ref, vcmp_ref):
    nch = kc_ref.shape[0] // CMP_STRIDE
    left = _left_half((nch, LANES))

    def one(x_ref, pe_ref, w1_ref, w2_ref, out_ref):
        u = [jnp.zeros((nch, CMP_HID), F32) for _ in range(C_KV)]
        v = [jnp.zeros((nch, CMP_HID), F32) for _ in range(C_KV)]
        for l in range(CMP_STRIDE):
            xl = x_ref[pl.ds(l, nch, stride=CMP_STRIDE), :]
            pe_u = pe_ref[l:l + 1, :]
            pe_v = pe_ref[CMP_STRIDE + l:CMP_STRIDE + l + 1, :]
            w_u = w1_ref[l]
            w_v = w1_ref[CMP_STRIDE + l]
            for g in range(C_KV):
                keep = left if g == 0 else jnp.logical_not(left)
                u[g] = u[g] + _dot(jnp.where(keep, xl + pe_u, 0.0).astype(BF16), w_u)
                v[g] = v[g] + _dot(jnp.where(keep, xl + pe_v, 0.0).astype(BF16), w_v)
        acc = jnp.zeros((nch, LANES), F32)
        for g in range(C_KV):
            hid = u[g] + pltpu.roll(v[g], nch - 1, 0)
            acc = acc + _dot(_silu(hid).astype(BF16), w2_ref[g])
        out_ref[...] = acc.astype(BF16)

    one(kc_ref, pek_ref, wk1_ref, wk2_ref, kcmp_ref)
    one(vc_ref, pev_ref, wv1_ref, wv2_ref, vcmp_ref)


def _compress(hg3, pek, pev, wk1, wk2, wv1, wv2):
    b, s, _ = hg3.shape
    nch = s // CMP_STRIDE
    full = lambda a: pl.BlockSpec(a.shape, lambda i: (0,) * a.ndim)
    return pl.pallas_call(
        _compress_kernel,
        grid=(b,),
        in_specs=[pl.BlockSpec((None, s, LANES), lambda i: (i, 0, G_KC // LANES)),
                  pl.BlockSpec((None, s, LANES), lambda i: (i, 0, G_VC // LANES)),
                  full(pek), full(pev), full(wk1), full(wk2), full(wv1), full(wv2)],
        out_specs=[pl.BlockSpec((None, nch, LANES), lambda i: (i, 0, 0)),
                   pl.BlockSpec((None, nch, LANES), lambda i: (i, 0, 0))],
        out_shape=[jax.ShapeDtypeStruct((b, nch, LANES), BF16)] * 2,
        compiler_params=_params("arbitrary"),
        name="compress",
    )(hg3, hg3, pek, pev, wk1, wk2, wv1, wv2)


def _stack_masked_q(q_ref, qs_ref, n_groups, width):
    total = q_ref.shape[1]
    grp = lax.broadcasted_iota(jnp.int32, (TQ, total), 1) // width
    q = q_ref[...]
    for c in range(n_groups):
        qs_ref[c * TQ:(c + 1) * TQ, :] = jnp.where(grp == c, q, jnp.zeros_like(q))


def _stack_paired_q(q_ref, qs_ref, n_blocks):
    left = _left_half((TQ, LANES))
    for i in range(n_blocks):
        q = q_ref[:, i * LANES:(i + 1) * LANES]
        zero = jnp.zeros_like(q)
        qs_ref[(2 * i) * TQ:(2 * i + 1) * TQ, :] = jnp.where(left, q, zero)
        qs_ref[(2 * i + 1) * TQ:(2 * i + 2) * TQ, :] = jnp.where(left, zero, q)


def _block_diag_v(v, left):
    zero = jnp.zeros_like(v)
    return jnp.concatenate([jnp.where(left, v, zero), jnp.where(left, zero, v)], axis=0)


def _online_update(sc, m_prev, l_prev):
    m_new = jnp.maximum(m_prev, jnp.max(sc, axis=1, keepdims=True))
    e = jnp.exp(sc - m_new)
    alpha = jnp.exp(m_prev - m_new)
    l_new = alpha * l_prev + jnp.sum(e, axis=1, keepdims=True)
    return e.astype(BF16), alpha, m_new, l_new


def _neg_dist_tile():
    r = lax.broadcasted_iota(jnp.int32, (TQ, TK), 0)
    c = lax.broadcasted_iota(jnp.int32, (TQ, TK), 1)
    return (c - r).astype(F32)


def _mixer_a_kernel(q_ref, k_ref, v_ref, g_ref, lq1_ref, lk1_ref, lq2_ref, lk2_ref, sub_ref,
                    y_ref, qs_ref, vblk_ref, m_ref, l_ref, acc_ref, *, lam_init):
    qi = pl.program_id(1)
    nmaps = 2
    ngrp = A_HEADS * nmaps
    nblk = A_W // LANES
    left_k = _left_half((TK, LANES))

    @pl.when(qi == 0)
    def _():
        for ch in range(v_ref.shape[0] // TK):
            for i in range(nblk):
                vblk_ref[ch, i] = _block_diag_v(v_ref[ch * TK:(ch + 1) * TK, i * LANES:(i + 1) * LANES], left_k)

    _stack_masked_q(q_ref, qs_ref, ngrp, A_QK)
    m_ref[...] = jnp.full(m_ref.shape, NEG, F32)
    l_ref[...] = jnp.zeros(l_ref.shape, F32)
    acc_ref[...] = jnp.zeros(acc_ref.shape, F32)
    t0 = qi * TQ
    negd = _neg_dist_tile()

    def body(ch, carry):
        s0 = pl.multiple_of(ch * TK, TK)
        s = _dot_nt(qs_ref[...], k_ref[pl.ds(s0, TK), :])
        nd = negd + (s0 - t0).astype(F32)
        causal = nd <= 0.0
        es, alphas = [], []
        for c in range(ngrp):
            sc = s[c * TQ:(c + 1) * TQ] + _SLOPE_A[c // nmaps] * nd
            sc = jnp.where(causal, sc, NEG)
            e, alpha, m_new, l_new = _online_update(sc, m_ref[c], l_ref[c])
            m_ref[c] = m_new
            l_ref[c] = l_new
            es.append(e)
            alphas.append(alpha)
        for mp in range(nmaps):
            for i in range(nblk):
                cl, cr = 2 * (2 * i) + mp, 2 * (2 * i + 1) + mp
                pv = _dot(jnp.concatenate([es[cl], es[cr]], axis=1), vblk_ref[ch, i])
                acc_ref[mp, i] = _pair(alphas[cl], alphas[cr], TQ) * acc_ref[mp, i] + pv
        return carry

    lax.fori_loop(0, (t0 + TQ + TK - 1) // TK, body, 0)

    lam = (jnp.exp(jnp.sum(lq1_ref[...] * lk1_ref[...], axis=1, keepdims=True))
           - jnp.exp(jnp.sum(lq2_ref[...] * lk2_ref[...], axis=1, keepdims=True)) + lam_init)
    left_q = _left_half((TQ, LANES))
    for i in range(nblk):
        o = []
        for mp in range(nmaps):
            cl, cr = 2 * (2 * i) + mp, 2 * (2 * i + 1) + mp
            o.append(acc_ref[mp, i] / _pair(l_ref[cl], l_ref[cr], TQ))
        d = o[0] - lam * o[1]
        sq = d * d
        ms_l = jnp.sum(jnp.where(left_q, sq, 0.0), axis=1, keepdims=True)
        ms_r = jnp.sum(jnp.where(left_q, 0.0, sq), axis=1, keepdims=True)
        ms = _pair(ms_l, ms_r, TQ) * (1.0 / HEAD_DIM)
        on = ((d * lax.rsqrt(ms + RMS_EPS)) * sub_ref[...]) * (1.0 - lam_init)
        y_ref[:, i * LANES:(i + 1) * LANES] = (on * _silu(g_ref[:, i * LANES:(i + 1) * LANES])).astype(BF16)


def _mixer_a(hq3, hg3, lq1, lk1, lq2, lk2, sub2, lam_init):
    b, s, _ = hq3.shape
    small = lambda a: pl.BlockSpec(a.shape, lambda bi, qi: (0,) * a.ndim)
    return pl.pallas_call(
        functools.partial(_mixer_a_kernel, lam_init=lam_init),
        grid=(b, s // TQ),
        in_specs=[pl.BlockSpec((None, TQ, A_W), lambda bi, qi: (bi, qi, Q_QA // A_W)),
                  pl.BlockSpec((None, s, A_W), lambda bi, qi: (bi, 0, Q_KA // A_W)),
                  pl.BlockSpec((None, s, A_W), lambda bi, qi: (bi, 0, Q_VA // A_W)),
                  pl.BlockSpec((None, TQ, A_W), lambda bi, qi: (bi, qi, G_GA // A_W)),
                  small(lq1), small(lk1), small(lq2), small(lk2), small(sub2)],
        out_specs=pl.BlockSpec((None, TQ, A_W), lambda bi, qi: (bi, qi, 0)),
        out_shape=jax.ShapeDtypeStruct((b, s, A_W), BF16),
        scratch_shapes=[pltpu.VMEM((2 * A_HEADS * TQ, A_W), BF16),
                        pltpu.VMEM((s // TK, A_W // LANES, 2 * TK, LANES), BF16),
                        pltpu.VMEM((2 * A_HEADS, TQ, 1), F32),
                        pltpu.VMEM((2 * A_HEADS, TQ, 1), F32),
                        pltpu.VMEM((2, A_W // LANES, TQ, LANES), F32)],
        compiler_params=_params("arbitrary", "arbitrary"),
        name="mixer_a",
    )(hq3, hq3, hq3, hg3, lq1, lk1, lq2, lk2, sub2)


def _mixer_b_kernel(sink_ref, q_ref, kp_ref, kc_ref, vp_ref, vc_ref, g_ref, y_ref, qs_ref):
    qi = pl.program_id(1)
    nblk = B_W // LANES
    ngrp = B_HEADS
    _stack_paired_q(q_ref, qs_ref, nblk)
    left_k = _left_half((TQ, LANES))
    r = lax.broadcasted_iota(jnp.int32, (TQ, TQ), 0)
    c_ = lax.broadcasted_iota(jnp.int32, (TQ, TQ), 1)
    negd = (c_ - r).astype(F32)
    first = jnp.where(qi > 0, 0, TQ)
    chunks = [
        (kp_ref, vp_ref, negd - float(TQ), c_ > r + first),
        (kc_ref, vc_ref, negd, c_ <= r),
    ]
    m = [jnp.full((TQ, 1), sink_ref[2 * (c % 2) + c // 2], F32) for c in range(ngrp)]
    l = [jnp.ones((TQ, 1), F32) for _ in range(ngrp)]
    acc = [jnp.zeros((TQ, LANES), F32) for _ in range(nblk)]
    for k_ref, v_ref, nd, valid in chunks:
        s = _dot_nt(qs_ref[...], k_ref[...])
        vblk = _block_diag_v(v_ref[...], left_k)
        es, alphas = [], []
        for c in range(ngrp):
            slope = _SLOPE_B[2 * (c % 2) + c // 2]
            sc = jnp.where(valid, s[c * TQ:(c + 1) * TQ] + slope * nd, NEG)
            e, alpha, m[c], l[c] = _online_update(sc, m[c], l[c])
            es.append(e)
            alphas.append(alpha)
        for i in range(nblk):
            pv = _dot(jnp.concatenate([es[2 * i], es[2 * i + 1]], axis=1), vblk)
            acc[i] = _pair(alphas[2 * i], alphas[2 * i + 1], TQ) * acc[i] + pv
    for i in range(nblk):
        o = acc[i] / _pair(l[2 * i], l[2 * i + 1], TQ)
        y_ref[:, i * LANES:(i + 1) * LANES] = (o * _silu(g_ref[:, i * LANES:(i + 1) * LANES])).astype(BF16)


def _mixer_b(sinks, hq3, hg3):
    b, s, _ = hq3.shape
    assert B_WINDOW == TQ
    prev = lambda bi, qi: jnp.maximum(qi - 1, 0)
    return pl.pallas_call(
        _mixer_b_kernel,
        grid=(b, s // TQ),
        in_specs=[pl.BlockSpec(memory_space=pltpu.SMEM),
                  pl.BlockSpec((None, TQ, B_W), lambda bi, qi: (bi, qi, Q_QB // B_W)),
                  pl.BlockSpec((None, TQ, KV_B), lambda bi, qi: (bi, prev(bi, qi), Q_KB // KV_B)),
                  pl.BlockSpec((None, TQ, KV_B), lambda bi, qi: (bi, qi, Q_KB // KV_B)),
                  pl.BlockSpec((None, TQ, KV_B), lambda bi, qi: (bi, prev(bi, qi), Q_VB // KV_B)),
                  pl.BlockSpec((None, TQ, KV_B), lambda bi, qi: (bi, qi, Q_VB // KV_B)),
                  pl.BlockSpec((None, TQ, B_W), lambda bi, qi: (bi, qi, G_GB // B_W))],
        out_specs=pl.BlockSpec((None, TQ, B_W), lambda bi, qi: (bi, qi, 0)),
        out_shape=jax.ShapeDtypeStruct((b, s, B_W), BF16),
        scratch_shapes=[pltpu.VMEM((B_HEADS * TQ, KV_B), BF16)],
        compiler_params=_params("arbitrary", "arbitrary"),
        name="mixer_b",
    )(sinks, hq3, hq3, hq3, hq3, hq3, hg3)


def _split3(x):
    hi = x.astype(BF16)
    r1 = x - hi.astype(F32)
    mid = r1.astype(BF16)
    lo = (r1 - mid.astype(F32)).astype(BF16)
    return hi, mid, lo


def _mixer_c_kernel(q_ref, gc_ref, gl_ref, kcmp_ref, vcmp_ref, ks_ref, vs_ref, kw_ref, vw_ref,
                    ovt_ref, exp_ref, y_ref,
                    qs_ref, vsblk_ref, vwblk_ref, vcblk_ref, m_ref, l_ref, acc_ref):
    qi = pl.program_id(1)
    nblk = C_W // LANES
    ngrp = C_HEADS
    s_len = ks_ref.shape[0]
    n_slc = s_len // SLC_LEN
    ncmp = kcmp_ref.shape[0]
    left_k = _left_half((TK, LANES))
    slope = [_SLOPE_C[(c % 2) * nblk + c // 2] for c in range(ngrp)]

    @pl.when(qi == 0)
    def _():
        for ch in range(s_len // TK):
            vsblk_ref[ch] = _block_diag_v(vs_ref[ch * TK:(ch + 1) * TK, :], left_k)
            vwblk_ref[ch] = _block_diag_v(vw_ref[ch * TK:(ch + 1) * TK, :], left_k)
        vcblk_ref[...] = _block_diag_v(vcmp_ref[...], _left_half((ncmp, LANES)))

    _stack_paired_q(q_ref, qs_ref, nblk)
    t0 = qi * TQ

    s = _dot_nt(qs_ref[...], kcmp_ref[...])
    tpos = t0 + lax.broadcasted_iota(jnp.int32, (TQ, ncmp), 0)
    cend = lax.broadcasted_iota(jnp.int32, (TQ, ncmp), 1) * CMP_STRIDE + (CMP_LEN - 1)
    dist_c = (tpos - cend).astype(F32)
    valid_c = dist_c >= 0.0
    p_cmp = []
    for c in range(ngrp):
        sc = jnp.where(valid_c, s[c * TQ:(c + 1) * TQ] - slope[c] * dist_c, NEG)
        mx = jnp.max(sc, axis=1, keepdims=True)
        e = jnp.where(valid_c, jnp.exp(sc - mx), 0.0)
        den = jnp.sum(e, axis=1, keepdims=True)
        p_cmp.append(e / jnp.where(den > 0.0, den, 1.0))
    o_cmp = [_dot(jnp.concatenate([p_cmp[2 * i], p_cmp[2 * i + 1]], axis=1).astype(BF16), vcblk_ref[...])
             for i in range(nblk)]

    jidx = lax.broadcasted_iota(jnp.int32, (n_slc, TQ), 0)
    tl = t0 + lax.broadcasted_iota(jnp.int32, (n_slc, TQ), 1)
    cur = tl // SLC_LEN
    causal_b = jidx * SLC_LEN <= tl
    forced = jnp.logical_or(jidx == 0, jnp.logical_or(jidx == cur, jidx == cur - 1))
    sel_t = []
    for g in range(C_KV):
        psum = p_cmp[g]
        for i in range(1, nblk):
            psum = psum + p_cmp[2 * i + g]
        imp = jnp.zeros((n_slc, TQ), F32)
        for part in _split3(psum):
            imp = imp + _dot_nt(ovt_ref[...], part)
        score = jnp.where(causal_b, jnp.where(forced, FORCE, imp), NEG)
        rank = jnp.zeros((n_slc, TQ), F32)
        for jp in range(n_slc):
            row = score[jp:jp + 1, :]
            tie = jnp.where(jidx > jp, 1.0, 0.0)
            rank = rank + jnp.where(row > score, 1.0, jnp.where(row == score, tie, 0.0))
        sel_t.append(jnp.where(rank < float(SLC_TOPN), 1.0, 0.0))
    pad = jnp.zeros((LANES - C_KV * n_slc, TQ), F32)
    sel_rows = jnp.transpose(jnp.concatenate(sel_t + [pad], axis=0)).astype(BF16)

    m_ref[...] = jnp.full(m_ref.shape, NEG, F32)
    l_ref[...] = jnp.zeros(l_ref.shape, F32)
    acc_ref[...] = jnp.zeros(acc_ref.shape, F32)
    negd = _neg_dist_tile()

    def flash(br, k_ref, vblk_ref, selected):
        def body(ch, carry):
            s0 = pl.multiple_of(ch * TK, TK)
            sk = _dot_nt(qs_ref[...], k_ref[pl.ds(s0, TK), :])
            nd = negd + (s0 - t0).astype(F32)
            if selected:
                ok = [jnp.where(nd <= 0.0, _dot(sel_rows, exp_ref[g, ch]), 0.0) > 0.5 for g in range(C_KV)]
            else:
                ok = [jnp.logical_and(nd <= 0.0, nd > -float(C_WINDOW))] * C_KV
            es, alphas = [], []
            for c in range(ngrp):
                sc = jnp.where(ok[c % 2], sk[c * TQ:(c + 1) * TQ] + slope[c] * nd, NEG)
                e, alpha, m_new, l_new = _online_update(sc, m_ref[br, c], l_ref[br, c])
                m_ref[br, c] = m_new
                l_ref[br, c] = l_new
                es.append(e)
                alphas.append(alpha)
            for i in range(nblk):
                pv = _dot(jnp.concatenate([es[2 * i], es[2 * i + 1]], axis=1), vblk_ref[ch])
                acc_ref[br, i] = _pair(alphas[2 * i], alphas[2 * i + 1], TQ) * acc_ref[br, i] + pv
            return carry
        return body

    n_end = (t0 + TQ + TK - 1) // TK
    lax.fori_loop(0, n_end, flash(0, ks_ref, vsblk_ref, True), 0)
    w_start = jnp.maximum(t0 - (C_WINDOW - 1), 0) // TK
    lax.fori_loop(w_start, n_end, flash(1, kw_ref, vwblk_ref, False), 0)

    gl = gl_ref[...]
    sig = 1.0 / (1.0 + jnp.exp(-gl))
    for i in range(nblk):
        cl, cr = 2 * i, 2 * i + 1

        def gate(br):
            return _pair(sig[:, br * ngrp + cl:br * ngrp + cl + 1], sig[:, br * ngrp + cr:br * ngrp + cr + 1], TQ)

        o_slc = acc_ref[0, i] / _pair(l_ref[0, cl], l_ref[0, cr], TQ)
        o_win = acc_ref[1, i] / _pair(l_ref[1, cl], l_ref[1, cr], TQ)
        oc = gate(0) * o_cmp[i] + gate(1) * o_slc + gate(2) * o_win
        y_ref[:, i * LANES:(i + 1) * LANES] = (oc * _silu(gc_ref[:, i * LANES:(i + 1) * LANES])).astype(BF16)


def _overlap_t(ncmp_pad, n_slc):
    cs = np.arange(ncmp_pad)[None, :] * CMP_STRIDE
    ss = np.arange(n_slc)[:, None] * SLC_LEN
    ov = np.minimum(cs + CMP_LEN, ss + SLC_LEN) - np.maximum(cs, ss)
    ov = np.maximum(ov, 0) / CMP_LEN
    ov[:, ncmp_pad - 1] = 0.0
    return ov.astype(np.float32)


def _expand_mat(s, n_slc):
    e = np.zeros((C_KV, s // TK, LANES, TK), dtype=np.float32)
    key_blk = (np.arange(s) // SLC_LEN).reshape(s // TK, TK)
    for g in range(C_KV):
        for j in range(n_slc):
            e[g, :, g * n_slc + j, :] = (key_blk == j)
    return e


def _mixer_c(hq3, hg3, kcmp, vcmp):
    b, s, _ = hq3.shape
    n_slc = s // SLC_LEN
    ncmp = kcmp.shape[1]
    assert C_KV * n_slc <= LANES
    ovt = jnp.asarray(_overlap_t(ncmp, n_slc), dtype=BF16)
    expand = jnp.asarray(_expand_mat(s, n_slc), dtype=BF16)
    kv = lambda off: pl.BlockSpec((None, s, KV_C), lambda bi, qi: (bi, 0, off // KV_C))
    return pl.pallas_call(
        _mixer_c_kernel,
        grid=(b, s // TQ),
        in_specs=[pl.BlockSpec((None, TQ, C_W), lambda bi, qi: (bi, qi, Q_QC // C_W)),
                  pl.BlockSpec((None, TQ, C_W), lambda bi, qi: (bi, qi, G_GC // C_W)),
                  pl.BlockSpec((None, TQ, LANES), lambda bi, qi: (bi, qi, G_GL // LANES)),
                  pl.BlockSpec((None, ncmp, LANES), lambda bi, qi: (bi, 0, 0)),
                  pl.BlockSpec((None, ncmp, LANES), lambda bi, qi: (bi, 0, 0)),
                  kv(Q_KS), kv(Q_VS), kv(Q_KW), kv(Q_VW),
                  pl.BlockSpec(ovt.shape, lambda bi, qi: (0, 0)),
                  pl.BlockSpec(expand.shape, lambda bi, qi: (0, 0, 0, 0))],
        out_specs=pl.BlockSpec((None, TQ, C_W), lambda bi, qi: (bi, qi, 0)),
        out_shape=jax.ShapeDtypeStruct((b, s, C_W), BF16),
        scratch_shapes=[pltpu.VMEM((C_HEADS * TQ, KV_C), BF16),
                        pltpu.VMEM((s // TK, 2 * TK, LANES), BF16),
                        pltpu.VMEM((s // TK, 2 * TK, LANES), BF16),
                        pltpu.VMEM((2 * ncmp, LANES), BF16),
                        pltpu.VMEM((2, C_HEADS, TQ, 1), F32),
                        pltpu.VMEM((2, C_HEADS, TQ, 1), F32),
                        pltpu.VMEM((2, C_W // LANES, TQ, LANES), F32)],
        compiler_params=_params("arbitrary", "arbitrary"),
        name="mixer_c",
    )(hq3, hg3, hg3, kcmp, vcmp, hq3, hq3, hq3, hq3, ovt, expand)


def kernel(x, norm_w, w_in, w_out, diff_lq1, diff_lk1, diff_lq2, diff_lk2, diff_subln, sinks,
           cmp_pe_k, cmp_pe_v, cmp_wk1, cmp_wk2, cmp_wv1, cmp_wv2, final_norm):
    b, s, d = x.shape
    depth = w_in.shape[0]
    assert s % TK == 0 and (b * s) % TM == 0 and d % LANES == 0

    w_in_p = (jnp.take(w_in, jnp.asarray(_IN_SRC), axis=2) * jnp.asarray(_IN_SCALE)).astype(BF16)
    w_out_p = jnp.take(w_out, jnp.asarray(_OUT_ROWS), axis=1).astype(BF16)
    pe2 = lambda pe: jnp.concatenate([pe, pe], axis=-1)
    w1 = lambda w: jnp.concatenate([w.reshape(depth, CMP_LEN, HEAD_DIM, CMP_HID)] * 2, axis=2).astype(BF16)
    zero2 = jnp.zeros((depth, CMP_HID, HEAD_DIM), F32)
    w2 = lambda w: jnp.stack([jnp.concatenate([w, zero2], axis=-1),
                              jnp.concatenate([zero2, w], axis=-1)], axis=1).astype(BF16)
    pek, pev = pe2(cmp_pe_k), pe2(cmp_pe_v)
    wk1, wv1 = w1(cmp_wk1), w1(cmp_wv1)
    wk2, wv2 = w2(cmp_wk2), w2(cmp_wv2)
    sub2 = jnp.concatenate([diff_subln, diff_subln], axis=-1)[:, None, :]
    fw = final_norm[None, :]

    x2 = x.reshape(b * s, d)
    for l in range(depth):
        hq, hg = _in_proj(x2, norm_w[l][None, :], w_in_p[l])
        hq3 = hq.reshape(b, s, NQ)
        hg3 = hg.reshape(b, s, NG)
        kcmp, vcmp = _compress(hg3, pek[l], pev[l], wk1[l], wk2[l], wv1[l], wv2[l])
        lam_init = 0.8 - 0.6 * math.exp(-0.3 * l)
        ya = _mixer_a(hq3, hg3, diff_lq1[l][None, :], diff_lk1[l][None, :], diff_lq2[l][None, :],
                      diff_lk2[l][None, :], sub2[l], lam_init)
        yb = _mixer_b(sinks[l], hq3, hg3)
        yc = _mixer_c(hq3, hg3, kcmp, vcmp)
        x2 = _out_proj(ya.reshape(b * s, A_W), yb.reshape(b * s, B_W), yc.reshape(b * s, C_W),
                       w_out_p[l], x2, fw, l == depth - 1)
    return x2.reshape(b, s, d)
```

```python
import functools
import math

import numpy as np
import jax
import jax.numpy as jnp
from jax import lax
from jax.experimental import pallas as pl
from jax.experimental.pallas import tpu as pltpu

F32 = jnp.float32
BF16 = jnp.bfloat16

HEAD_DIM = 64
RMS_EPS = 1e-6
NEG = -1e30
FORCE = 1e30

A_HEADS = 4
A_QK = HEAD_DIM // 2
B_HEADS = 4
B_KV = 2
B_WINDOW = 128
C_HEADS = 8
C_KV = 2
CMP_LEN = 32
CMP_STRIDE = 16
CMP_HID = 128
SLC_LEN = 64
SLC_TOPN = 8
C_WINDOW = 512

A_W = A_HEADS * HEAD_DIM
B_W = B_HEADS * HEAD_DIM
C_W = C_HEADS * HEAD_DIM
KV_B = B_KV * HEAD_DIM
KV_C = C_KV * HEAD_DIM
N_GATES = 3 * C_HEADS

LANES = 128
SUBLANES = 8
BF16_ROWS = 16
TQ = 128
TK = 256
TM = 512
AHEAD = 4
V_ROWS = HEAD_DIM + BF16_ROWS
VMEM_LIMIT = 48 * 1024 * 1024

_O = dict(qa=0, ka=256, va=512, ga=768, qb=1024, kb=1280, vb=1408, gb=1536, qc=1792,
          kc=2304, vc=2432, ks=2560, vs=2688, kw=2816, vw=2944, gl=3072, gc=3096)

Q_QC, Q_QA, Q_KA, Q_QB, Q_KB, Q_KS, Q_KW = 0, 512, 768, 1024, 1280, 1408, 1536
NQ = 1664
G_GC, G_GA, G_GB, G_KC, G_VC = 0, 512, 768, 1024, 1152
NG = 1280
T_VA, T_VB, T_VS, T_VW = 0, 256, 384, 512
NVT = 640
NGT = 32


LOG2E = math.log2(math.e)


def _slopes(n):
    return [float(2.0 ** (-8.0 * h / n)) * LOG2E for h in range(1, n + 1)]


_SLOPE_A = _slopes(A_HEADS)
_SLOPE_B = _slopes(B_HEADS)
_SLOPE_C = _slopes(C_HEADS)


def _params(*sem):
    return pltpu.CompilerParams(dimension_semantics=sem, vmem_limit_bytes=VMEM_LIMIT)


def _dot_nt(a, b):
    return lax.dot_general(a, b, (((1,), (1,)), ((), ())), preferred_element_type=F32)


def _dot(a, b):
    return jnp.dot(a, b, preferred_element_type=F32)


def _silu(g):
    return g * (1.0 / (1.0 + jnp.exp(-g)))


def _left_half(shape):
    return lax.broadcasted_iota(jnp.int32, shape, len(shape) - 1) < HEAD_DIM


def _pair_cols(w, n_pairs):
    lead = w.shape[:-1]
    w = w.reshape(lead + (2, n_pairs, HEAD_DIM))
    return jnp.swapaxes(w, -3, -2).reshape(lead + (2 * n_pairs * HEAD_DIM,))


def _prep_w_in(w_in):
    depth, d, _ = w_in.shape
    col = lambda name, width: w_in[:, :, _O[name]:_O[name] + width]
    rows = [
        _pair_cols(col('qc', C_W), C_HEADS // C_KV) * (HEAD_DIM ** -0.5 * LOG2E),
        col('qa', A_W) * (A_QK ** -0.5 * LOG2E),
        col('ka', A_W),
        _pair_cols(col('qb', B_W), B_HEADS // B_KV) * (HEAD_DIM ** -0.5 * LOG2E),
        col('kb', KV_B), col('ks', KV_C), col('kw', KV_C),
        _pair_cols(col('gc', C_W), C_HEADS // C_KV),
        col('ga', A_W),
        _pair_cols(col('gb', B_W), B_HEADS // B_KV),
        col('kc', KV_C), col('vc', KV_C),
    ]
    w_rm = jnp.concatenate(rows, axis=2).astype(BF16)
    gl = col('gl', N_GATES).reshape(depth, d, C_KV, C_HEADS // C_KV, 3)
    gl = jnp.transpose(gl, (0, 1, 4, 3, 2)).reshape(depth, d, N_GATES)
    gl = jnp.concatenate([gl, jnp.zeros((depth, d, NGT - N_GATES), w_in.dtype)], axis=2)
    cols_t = jnp.concatenate([col('va', A_W), col('vb', KV_B), col('vs', KV_C), col('vw', KV_C), gl], axis=2)
    w_t = jnp.swapaxes(cols_t, 1, 2).astype(BF16)
    return w_rm, w_t


def _prep_w_out(w_out):
    parts = [w_out[:, :A_W],
             jnp.swapaxes(_pair_cols(jnp.swapaxes(w_out[:, A_W:A_W + B_W], 1, 2), B_HEADS // B_KV), 1, 2),
             jnp.swapaxes(_pair_cols(jnp.swapaxes(w_out[:, A_W + B_W:], 1, 2), C_HEADS // C_KV), 1, 2)]
    return jnp.concatenate(parts, axis=1).astype(BF16)


def _snd_tiles(slopes):
    nd = np.arange(TK)[:, None] - np.arange(TQ)[None, :]
    return jnp.asarray(np.stack([s * nd for s in slopes]).astype(np.float32))


def _in_proj_kernel(x_ref, nw_ref, w_ref, wt_ref, hq_ref, hg_ref, vt_ref, gt_ref):
    x = x_ref[...]
    ms = jnp.mean(x * x, axis=-1, keepdims=True)
    xn = ((x * lax.rsqrt(ms + RMS_EPS)) * nw_ref[...]).astype(BF16)
    step = 256
    for c0 in range(0, NQ, step):
        c1 = min(c0 + step, NQ)
        hq_ref[:, c0:c1] = _dot(xn, w_ref[:, c0:c1]).astype(BF16)
    for c0 in range(0, NG, step):
        c1 = min(c0 + step, NG)
        hg_ref[:, c0:c1] = _dot(xn, w_ref[:, NQ + c0:NQ + c1])
    for r0 in range(0, NVT, LANES):
        vt_ref[r0:r0 + LANES, :] = _dot_nt(wt_ref[r0:r0 + LANES, :], xn).astype(BF16)
    gt_ref[...] = _dot_nt(wt_ref[NVT:NVT + NGT, :], xn)


def _in_proj(x2, nw, w_rm, w_t):
    n, d = x2.shape
    return pl.pallas_call(
        _in_proj_kernel,
        grid=(n // TM,),
        in_specs=[pl.BlockSpec((TM, d), lambda i: (i, 0)),
                  pl.BlockSpec((1, d), lambda i: (0, 0)),
                  pl.BlockSpec((d, NQ + NG), lambda i: (0, 0)),
                  pl.BlockSpec((NVT + NGT, d), lambda i: (0, 0))],
        out_specs=[pl.BlockSpec((TM, NQ), lambda i: (i, 0)),
                   pl.BlockSpec((TM, NG), lambda i: (i, 0)),
                   pl.BlockSpec((NVT, TM), lambda i: (0, i)),
                   pl.BlockSpec((NGT, TM), lambda i: (0, i))],
        out_shape=[jax.ShapeDtypeStruct((n, NQ), BF16), jax.ShapeDtypeStruct((n, NG), F32),
                   jax.ShapeDtypeStruct((NVT, n), BF16), jax.ShapeDtypeStruct((NGT, n), F32)],
        compiler_params=_params("arbitrary"),
        name="in_proj",
    )(x2, nw, w_rm, w_t)


def _out_proj_kernel(ya_ref, yb_ref, yc_ref, w_ref, x_ref, fw_ref, o_ref, *, final):
    acc = x_ref[...]
    acc = acc + _dot(ya_ref[...], w_ref[0:A_W, :])
    acc = acc + _dot(yb_ref[...], w_ref[A_W:A_W + B_W, :])
    acc = acc + _dot(yc_ref[...], w_ref[A_W + B_W:, :])
    if final:
        ms = jnp.mean(acc * acc, axis=-1, keepdims=True)
        acc = (acc * lax.rsqrt(ms + RMS_EPS)) * fw_ref[...]
    o_ref[...] = acc


def _out_proj(ya, yb, yc, w, x2, fw, final):
    n, d = x2.shape
    return pl.pallas_call(
        functools.partial(_out_proj_kernel, final=final),
        grid=(n // TM,),
        in_specs=[pl.BlockSpec((TM, A_W), lambda i: (i, 0)),
                  pl.BlockSpec((TM, B_W), lambda i: (i, 0)),
                  pl.BlockSpec((TM, C_W), lambda i: (i, 0)),
                  pl.BlockSpec((A_W + B_W + C_W, d), lambda i: (0, 0)),
                  pl.BlockSpec((TM, d), lambda i: (i, 0)),
                  pl.BlockSpec((1, d), lambda i: (0, 0))],
        out_specs=pl.BlockSpec((TM, d), lambda i: (i, 0)),
        out_shape=jax.ShapeDtypeStruct((n, d), F32),
        compiler_params=_params("arbitrary"),
        name="out_proj",
    )(ya, yb, yc, w, x2, fw)


def _compress_kernel(kc_ref, vc_ref, pek_ref, pev_ref, wk1_ref, wk2_ref, wv1_ref, wv2_ref,
                     kcmp_ref, vcmpt_ref):
    nch = kc_ref.shape[0] // CMP_STRIDE
    left = _left_half((nch, LANES))

    def one(x_ref, pe_ref, w1_ref, w2_ref, out_ref, transposed):
        u = [jnp.zeros((nch, CMP_HID), F32) for _ in range(C_KV)]
        v = [jnp.zeros((nch, CMP_HID), F32) for _ in range(C_KV)]
        for l in range(CMP_STRIDE):
            xl = x_ref[pl.ds(l, nch, stride=CMP_STRIDE), :]
            pe_u = pe_ref[l:l + 1, :]
            pe_v = pe_ref[CMP_STRIDE + l:CMP_STRIDE + l + 1, :]
            w_u = w1_ref[l]
            w_v = w1_ref[CMP_STRIDE + l]
            for g in range(C_KV):
                keep = left if g == 0 else jnp.logical_not(left)
                u[g] = u[g] + _dot(jnp.where(keep, xl + pe_u, 0.0).astype(BF16), w_u)
                v[g] = v[g] + _dot(jnp.where(keep, xl + pe_v, 0.0).astype(BF16), w_v)
        acc = jnp.zeros((LANES, nch) if transposed else (nch, LANES), F32)
        for g in range(C_KV):
            hid = u[g] + pltpu.roll(v[g], nch - 1, 0)
            act = _silu(hid).astype(BF16)
            acc = acc + (_dot_nt(w2_ref[g], act) if transposed else _dot(act, w2_ref[g]))
        out_ref[...] = acc.astype(BF16)

    one(kc_ref, pek_ref, wk1_ref, wk2_ref, kcmp_ref, False)
    one(vc_ref, pev_ref, wv1_ref, wv2_ref, vcmpt_ref, True)


def _compress(hg3, pek, pev, wk1, wk2, wv1, wv2t):
    b, s, _ = hg3.shape
    nch = s // CMP_STRIDE
    full = lambda a: pl.BlockSpec(a.shape, lambda i: (0,) * a.ndim)
    return pl.pallas_call(
        _compress_kernel,
        grid=(b,),
        in_specs=[pl.BlockSpec((None, s, LANES), lambda i: (i, 0, G_KC // LANES)),
                  pl.BlockSpec((None, s, LANES), lambda i: (i, 0, G_VC // LANES)),
                  full(pek), full(pev), full(wk1), full(wk2), full(wv1), full(wv2t)],
        out_specs=[pl.BlockSpec((None, nch, LANES), lambda i: (i, 0, 0)),
                   pl.BlockSpec((None, LANES, nch), lambda i: (i, 0, 0))],
        out_shape=[jax.ShapeDtypeStruct((b, nch, LANES), BF16), jax.ShapeDtypeStruct((b, LANES, nch), BF16)],
        compiler_params=_params("arbitrary"),
        name="compress",
    )(hg3, hg3, pek, pev, wk1, wk2, wv1, wv2t)


def _stack_masked_q(q_ref, qs_ref, n_groups, width):
    per_block = LANES // width
    sub = lax.broadcasted_iota(jnp.int32, (TQ, LANES), 1) // width
    for c in range(n_groups):
        blk = c // per_block
        q = q_ref[:, blk * LANES:(blk + 1) * LANES]
        qs_ref[c * TQ:(c + 1) * TQ, :] = jnp.where(sub == c % per_block, q, jnp.zeros_like(q))


def _stack_paired_q(q_ref, qs_ref, n_blocks):
    left = _left_half((TQ, LANES))
    for i in range(n_blocks):
        q = q_ref[:, i * LANES:(i + 1) * LANES]
        zero = jnp.zeros_like(q)
        qs_ref[i * TQ:(i + 1) * TQ, :] = jnp.where(left, q, zero)
        qs_ref[(n_blocks + i) * TQ:(n_blocks + i + 1) * TQ, :] = jnp.where(left, zero, q)


def _fill_value_rows(vaug_ref, vt_ref, n_heads):
    s = vt_ref.shape[1]
    for h in range(n_heads):
        vaug_ref[h, 0:HEAD_DIM, :] = vt_ref[h * HEAD_DIM:(h + 1) * HEAD_DIM, :]
        vaug_ref[h, HEAD_DIM:V_ROWS, :] = jnp.ones((BF16_ROWS, s), BF16)


def _key_minus_query():
    r = lax.broadcasted_iota(jnp.int32, (TK, TQ), 0)
    c = lax.broadcasted_iota(jnp.int32, (TK, TQ), 1)
    return (r - c).astype(F32)


def _pair_scores(k_chunk, qs_ref, c0):
    return _dot_nt(k_chunk, qs_ref[c0 * TQ:(c0 + 2) * TQ, :])


def _flash_pair(st, c0, snd_ref, slopes, kappa, madd, m_ref, e_ref):
    alphas = []
    for j in range(2):
        c = c0 + j
        u = st[:, j * TQ:(j + 1) * TQ] + snd_ref[c]
        if madd is not None:
            u = u + madd
        sigma = slopes[c] * kappa
        m_old = m_ref[c:c + 1, :]
        m_new = jnp.maximum(m_old, jnp.max(u, axis=0, keepdims=True) + sigma)
        e_ref[:, c * TQ:(c + 1) * TQ] = jnp.exp2(u - (m_new - sigma)).astype(BF16)
        alphas.append(jnp.exp2(m_old - m_new))
        m_ref[c:c + 1, :] = m_new
    return alphas


def _accumulate(acc_ref, idx, vaug_chunk, e_ref, c0, alphas):
    a = jnp.concatenate(alphas, axis=1)
    acc_ref[idx] = acc_ref[idx] * a + _dot(vaug_chunk, e_ref[:, c0 * TQ:(c0 + 2) * TQ])


def _pipelined(items, scores, consume, st_ref=None, lookahead=()):
    if st_ref is None:
        queue = [scores(item) for item in items[:AHEAD]]
    else:
        assert len(items) >= AHEAD and len(lookahead) in (0, AHEAD)
        queue = [functools.partial(lambda j: st_ref[j], j) for j in range(AHEAD)]
    todo = list(items[AHEAD:]) + list(lookahead)
    for item in items:
        if todo:
            queue.append(scores(todo.pop(0)))
        st = queue.pop(0)
        consume(item, st() if callable(st) else st)
    for j, st in enumerate(queue):
        st_ref[j] = st


def _normalized(acc, lane0):
    return acc[0:HEAD_DIM, lane0:lane0 + TQ] / acc[HEAD_DIM:HEAD_DIM + 1, lane0:lane0 + TQ]


def _mixer_a_kernel(q_ref, k_ref, vt_ref, g_ref, lq1_ref, lk1_ref, lq2_ref, lk2_ref, sub_ref, snd_ref,
                    y_ref, qs_ref, vaug_ref, e_ref, acc_ref, m_ref, st_ref, *, lam_init):
    qi = pl.program_id(1)
    ngrp = 2 * A_HEADS
    slopes = [_SLOPE_A[c // 2] for c in range(ngrp)]

    @pl.when(qi == 0)
    def _():
        _fill_value_rows(vaug_ref, vt_ref, A_HEADS)

    _stack_masked_q(q_ref, qs_ref, ngrp, A_QK)
    acc_ref[...] = jnp.zeros(acc_ref.shape, F32)
    m_ref[...] = jnp.full(m_ref.shape, NEG, F32)
    t0 = qi * TQ
    kmq = _key_minus_query()

    def start(ch):
        return pl.multiple_of(ch * TK, TK)

    def run(chunks, next_chunk=None):
        def scores(item):
            ch, _, h = item
            blk = h * HEAD_DIM // LANES
            return _pair_scores(k_ref[pl.ds(start(ch), TK), blk * LANES:(blk + 1) * LANES], qs_ref, 2 * h)

        def consume(item, st):
            ch, diagonal, h = item
            s0 = start(ch)
            kappa = (s0 - t0).astype(F32)
            madd = jnp.where(kmq + kappa <= 0.0, 0.0, NEG) if diagonal else None
            alphas = _flash_pair(st, 2 * h, snd_ref, slopes, kappa, madd, m_ref, e_ref)
            _accumulate(acc_ref, h, vaug_ref[h, :, pl.ds(s0, TK)], e_ref, 2 * h, alphas)

        ahead = [] if next_chunk is None else [(next_chunk, False, h) for h in range(AHEAD)]
        if chunks:
            _pipelined([(ch, dg, h) for ch, dg in chunks for h in range(A_HEADS)], scores, consume, st_ref, ahead)
        else:
            for j, item in enumerate(ahead):
                st_ref[j] = scores(item)

    last = (t0 + TQ - 1) // TK
    run([], 0)

    def two_chunks(i, carry):
        run([(2 * i, False), (2 * i + 1, False)], 2 * i + 2)
        return carry

    lax.fori_loop(0, last // 2, two_chunks, 0)

    @pl.when(last % 2 == 1)
    def _():
        run([(last - 1, False), (last, True)])

    @pl.when(last % 2 == 0)
    def _():
        run([(last, True)])

    lam = (jnp.exp(jnp.sum(lq1_ref[...] * lk1_ref[...], axis=1, keepdims=True))
           - jnp.exp(jnp.sum(lq2_ref[...] * lk2_ref[...], axis=1, keepdims=True)) + lam_init)
    outs = []
    for h in range(A_HEADS):
        acc = acc_ref[h]
        d = _normalized(acc, 0) - lam * _normalized(acc, TQ)
        ms = jnp.mean(d * d, axis=0, keepdims=True)
        outs.append(((d * lax.rsqrt(ms + RMS_EPS)) * sub_ref[...]) * (1.0 - lam_init))
    for i in range(A_W // LANES):
        o = jnp.transpose(jnp.concatenate(outs[2 * i:2 * i + 2], axis=0))
        y_ref[:, i * LANES:(i + 1) * LANES] = (o * _silu(g_ref[:, i * LANES:(i + 1) * LANES])).astype(BF16)


def _mixer_a(hq3, hg3, vt, lq1, lk1, lq2, lk2, sub, lam_init):
    b, s, _ = hq3.shape
    snd = _snd_tiles([_SLOPE_A[c // 2] for c in range(2 * A_HEADS)])
    small = lambda a: pl.BlockSpec(a.shape, lambda bi, qi: (0,) * a.ndim)
    return pl.pallas_call(
        functools.partial(_mixer_a_kernel, lam_init=lam_init),
        grid=(b, s // TQ),
        in_specs=[pl.BlockSpec((None, TQ, A_W), lambda bi, qi: (bi, qi, Q_QA // A_W)),
                  pl.BlockSpec((None, s, A_W), lambda bi, qi: (bi, 0, Q_KA // A_W)),
                  pl.BlockSpec((A_W, s), lambda bi, qi: (T_VA // A_W, bi)),
                  pl.BlockSpec((None, TQ, A_W), lambda bi, qi: (bi, qi, G_GA // A_W)),
                  small(lq1), small(lk1), small(lq2), small(lk2), small(sub), small(snd)],
        out_specs=pl.BlockSpec((None, TQ, A_W), lambda bi, qi: (bi, qi, 0)),
        out_shape=jax.ShapeDtypeStruct((b, s, A_W), BF16),
        scratch_shapes=[pltpu.VMEM((2 * A_HEADS * TQ, LANES), BF16),
                        pltpu.VMEM((A_HEADS, V_ROWS, s), BF16),
                        pltpu.VMEM((TK, 2 * A_HEADS * TQ), BF16),
                        pltpu.VMEM((A_HEADS, V_ROWS, 2 * TQ), F32),
                        pltpu.VMEM((2 * A_HEADS, TQ), F32),
                        pltpu.VMEM((AHEAD, TK, 2 * TQ), F32)],
        compiler_params=_params("arbitrary", "arbitrary"),
        name="mixer_a",
    )(hq3, hq3, vt, hg3, lq1, lk1, lq2, lk2, sub, snd)


def _mixer_b_kernel(sink_ref, q_ref, k_ref, vt_ref, g_ref, snd_ref, y_ref, qs_ref, vaug_ref, e_ref, m_ref):
    qi = pl.program_id(1)
    nblk = B_W // LANES
    ngrp = B_HEADS

    @pl.when(qi == 0)
    def _():
        _fill_value_rows(vaug_ref, vt_ref, B_KV)

    _stack_paired_q(q_ref, qs_ref, nblk)
    t0 = qi * TQ
    s0 = pl.multiple_of(jnp.maximum(t0 - B_WINDOW, 0), LANES)
    k_chunk = k_ref[pl.ds(s0, TK), :]
    kappa = (s0 - t0).astype(F32)
    nd = _key_minus_query() + kappa
    madd = jnp.where(jnp.logical_and(nd <= 0.0, nd > -float(B_WINDOW)), 0.0, NEG)
    for c in range(ngrp):
        m_ref[c:c + 1, :] = jnp.full((1, TQ), sink_ref[c] * LOG2E, F32)
    outs = []

    def consume(g, st):
        alphas = _flash_pair(st, g * nblk, snd_ref, _SLOPE_B, kappa, madd, m_ref, e_ref)
        acc = _dot(vaug_ref[g, :, pl.ds(s0, TK)], e_ref[:, g * nblk * TQ:(g + 1) * nblk * TQ])
        for i in range(nblk):
            lanes = slice(i * TQ, (i + 1) * TQ)
            den = acc[HEAD_DIM:HEAD_DIM + 1, lanes] + alphas[i]
            outs.append(acc[0:HEAD_DIM, lanes] / den)

    _pipelined(list(range(B_KV)), lambda g: _pair_scores(k_chunk, qs_ref, g * nblk), consume)
    for i in range(nblk):
        o = jnp.transpose(jnp.concatenate([outs[i], outs[nblk + i]], axis=0))
        y_ref[:, i * LANES:(i + 1) * LANES] = (o * _silu(g_ref[:, i * LANES:(i + 1) * LANES])).astype(BF16)


def _mixer_b(sinks, hq3, hg3, vt):
    b, s, _ = hq3.shape
    assert B_WINDOW + TQ <= TK
    snd = _snd_tiles(_SLOPE_B)
    return pl.pallas_call(
        _mixer_b_kernel,
        grid=(b, s // TQ),
        in_specs=[pl.BlockSpec(memory_space=pltpu.SMEM),
                  pl.BlockSpec((None, TQ, B_W), lambda bi, qi: (bi, qi, Q_QB // B_W)),
                  pl.BlockSpec((None, s, KV_B), lambda bi, qi: (bi, 0, Q_KB // KV_B)),
                  pl.BlockSpec((KV_B, s), lambda bi, qi: (T_VB // KV_B, bi)),
                  pl.BlockSpec((None, TQ, B_W), lambda bi, qi: (bi, qi, G_GB // B_W)),
                  pl.BlockSpec(snd.shape, lambda bi, qi: (0, 0, 0))],
        out_specs=pl.BlockSpec((None, TQ, B_W), lambda bi, qi: (bi, qi, 0)),
        out_shape=jax.ShapeDtypeStruct((b, s, B_W), BF16),
        scratch_shapes=[pltpu.VMEM((B_HEADS * TQ, KV_B), BF16),
                        pltpu.VMEM((B_KV, V_ROWS, s), BF16),
                        pltpu.VMEM((TK, B_HEADS * TQ), BF16),
                        pltpu.VMEM((SUBLANES, TQ), F32)],
        compiler_params=_params("arbitrary", "arbitrary"),
        name="mixer_b",
    )(sinks, hq3, hq3, vt, hg3, snd)


def _split3(x):
    hi = x.astype(BF16)
    r1 = x - hi.astype(F32)
    mid = r1.astype(BF16)
    lo = (r1 - mid.astype(F32)).astype(BF16)
    return hi, mid, lo


def _mixer_c_kernel(q_ref, gc_ref, glt_ref, kcmp_ref, vcmpt_ref, ks_ref, vst_ref, kw_ref, vwt_ref,
                    ovt_ref, snd_ref, y_ref,
                    qs_ref, vaug_ref, e_ref, acc_ref, msel_ref, m_ref, st_ref):
    qi = pl.program_id(1)
    nblk = C_W // LANES
    ngrp = C_HEADS
    s_len = ks_ref.shape[0]
    n_slc = s_len // SLC_LEN
    ncmp = kcmp_ref.shape[0]
    blocks_per_chunk = TK // SLC_LEN

    @pl.when(qi == 0)
    def _():
        for g in range(C_KV):
            grp = slice(g * HEAD_DIM, (g + 1) * HEAD_DIM)
            vaug_ref[0, g, 0:HEAD_DIM, :] = vst_ref[grp, :]
            vaug_ref[1, g, 0:HEAD_DIM, :] = vwt_ref[grp, :]
            for br in range(2):
                vaug_ref[br, g, HEAD_DIM:V_ROWS, :] = jnp.ones((BF16_ROWS, s_len), BF16)

    _stack_paired_q(q_ref, qs_ref, nblk)
    t0 = qi * TQ

    st = _dot_nt(kcmp_ref[...], qs_ref[...])
    tpos = t0 + lax.broadcasted_iota(jnp.int32, (ncmp, TQ), 1)
    cend = lax.broadcasted_iota(jnp.int32, (ncmp, TQ), 0) * CMP_STRIDE + (CMP_LEN - 1)
    dist_c = (tpos - cend).astype(F32)
    valid_c = dist_c >= 0.0
    p_cmp = []
    for c in range(ngrp):
        sc = jnp.where(valid_c, st[:, c * TQ:(c + 1) * TQ] - _SLOPE_C[c] * dist_c, NEG)
        mx = jnp.max(sc, axis=0, keepdims=True)
        e = jnp.where(valid_c, jnp.exp2(sc - mx), 0.0)
        den = jnp.sum(e, axis=0, keepdims=True)
        p_cmp.append(e / jnp.where(den > 0.0, den, 1.0))
    o_cmp = [_dot(vcmpt_ref[g * HEAD_DIM:(g + 1) * HEAD_DIM, :],
                  jnp.concatenate(p_cmp[g * nblk:(g + 1) * nblk], axis=1).astype(BF16))
             for g in range(C_KV)]

    jidx = lax.broadcasted_iota(jnp.int32, (n_slc, TQ), 0)
    tl = t0 + lax.broadcasted_iota(jnp.int32, (n_slc, TQ), 1)
    cur = tl // SLC_LEN
    causal_b = jidx * SLC_LEN <= tl
    forced = jnp.logical_or(jidx == 0, jnp.logical_or(jidx == cur, jidx == cur - 1))
    for g in range(C_KV):
        psum = p_cmp[g * nblk]
        for i in range(1, nblk):
            psum = psum + p_cmp[g * nblk + i]
        imp = jnp.zeros((n_slc, TQ), F32)
        for part in _split3(psum):
            imp = imp + _dot(ovt_ref[...], part)
        score = jnp.where(causal_b, jnp.where(forced, FORCE, imp), NEG)
        rank = jnp.zeros((n_slc, TQ), F32)
        for jp in range(n_slc):
            row = score[jp:jp + 1, :]
            tie = jnp.where(jidx > jp, 1.0, 0.0)
            rank = rank + jnp.where(row > score, 1.0, jnp.where(row == score, tie, 0.0))
        msel_ref[g] = jnp.where(rank < float(SLC_TOPN), 0.0, NEG)

    acc_ref[...] = jnp.zeros(acc_ref.shape, F32)
    m_ref[...] = jnp.full(m_ref.shape, NEG, F32)
    kmq = _key_minus_query()

    def run(chunks, next_chunk=None):
        masks = {}

        def start(ch):
            return pl.multiple_of(ch * TK, TK)

        def scores(item):
            n, p = item
            br, ch, _ = chunks[n]
            k_ref = ks_ref if br == 0 else kw_ref
            return _pair_scores(k_ref[pl.ds(start(ch), TK), :], qs_ref, 2 * p)

        def mask(n, g):
            br, ch, extra = chunks[n]
            key = (n, g if br == 0 else 0)
            if key not in masks:
                nd = kmq + (start(ch) - t0).astype(F32)
                if br == 0:
                    rows = [jnp.broadcast_to(msel_ref[g, pl.ds(ch * blocks_per_chunk + j, 1), :], (SLC_LEN, TQ))
                            for j in range(blocks_per_chunk)]
                    madd = jnp.concatenate(rows, axis=0)
                    if extra:
                        madd = madd + jnp.where(nd <= 0.0, 0.0, NEG)
                else:
                    madd = jnp.where(jnp.logical_and(nd <= 0.0, nd > extra), 0.0, NEG)
                masks[key] = madd
            return masks[key]

        def consume(item, st):
            n, p = item
            br, ch, _ = chunks[n]
            g = 2 * p // nblk
            s0 = start(ch)
            kappa = (s0 - t0).astype(F32)
            alphas = _flash_pair(st, 2 * p, snd_ref, _SLOPE_C, kappa, mask(n, g), m_ref.at[br], e_ref)
            _accumulate(acc_ref.at[br], p, vaug_ref[br, g, :, pl.ds(s0, TK)], e_ref, 2 * p, alphas)

        n_run = len(chunks)
        ahead = []
        if next_chunk is not None:
            chunks = chunks + [(0, next_chunk, False)]
            ahead = [(n_run, p) for p in range(AHEAD)]
        if n_run:
            _pipelined([(n, p) for n in range(n_run) for p in range(ngrp // 2)], scores, consume, st_ref, ahead)
        else:
            for j, item in enumerate(ahead):
                st_ref[j] = scores(item)

    last = (t0 + TQ - 1) // TK
    run([], 0)

    def two_chunks(i, carry):
        run([(0, 2 * i, False), (0, 2 * i + 1, False)], 2 * i + 2)
        return carry

    lax.fori_loop(0, last // 2, two_chunks, 0)

    n_win = max((t + TQ - 1) // TK - (t - C_WINDOW + 1) // TK + 1 for t in range(0, TK * TQ, TQ))
    window = [(1, jnp.maximum(last - r, 0), jnp.where(last - r >= 0, -float(C_WINDOW), -NEG))
              for r in reversed(range(n_win))]

    @pl.when(last % 2 == 1)
    def _():
        run([(0, last - 1, False), (0, last, True)] + window)

    @pl.when(last % 2 == 0)
    def _():
        run([(0, last, True)] + window)

    sig = 1.0 / (1.0 + jnp.exp(-glt_ref[...]))
    outs = []
    for g in range(C_KV):
        for i in range(nblk):
            c = g * nblk + i
            a_slc = acc_ref[0, c // 2]
            a_win = acc_ref[1, c // 2]
            lane0 = (c % 2) * TQ
            gate = lambda br: sig[br * ngrp + 2 * i + g:br * ngrp + 2 * i + g + 1, :]
            outs.append(gate(0) * o_cmp[g][:, i * TQ:(i + 1) * TQ]
                        + gate(1) * _normalized(a_slc, lane0) + gate(2) * _normalized(a_win, lane0))
    for i in range(nblk):
        o = jnp.transpose(jnp.concatenate([outs[i], outs[nblk + i]], axis=0))
        y_ref[:, i * LANES:(i + 1) * LANES] = (o * _silu(gc_ref[:, i * LANES:(i + 1) * LANES])).astype(BF16)


def _overlap_t(ncmp_pad, n_slc):
    cs = np.arange(ncmp_pad)[None, :] * CMP_STRIDE
    ss = np.arange(n_slc)[:, None] * SLC_LEN
    ov = np.minimum(cs + CMP_LEN, ss + SLC_LEN) - np.maximum(cs, ss)
    ov = np.maximum(ov, 0) / CMP_LEN
    ov[:, ncmp_pad - 1] = 0.0
    return ov.astype(np.float32)


def _mixer_c(hq3, hg3, vt, gt, kcmp, vcmpt):
    b, s, _ = hq3.shape
    n_slc = s // SLC_LEN
    ncmp = kcmp.shape[1]
    nq = s // TQ
    ovt = jnp.asarray(_overlap_t(ncmp, n_slc), dtype=BF16)
    snd = _snd_tiles(_SLOPE_C)
    krow = lambda off: pl.BlockSpec((None, s, KV_C), lambda bi, qi: (bi, 0, off // KV_C))
    vrow = lambda off: pl.BlockSpec((KV_C, s), lambda bi, qi: (off // KV_C, bi))
    return pl.pallas_call(
        _mixer_c_kernel,
        grid=(b, nq),
        in_specs=[pl.BlockSpec((None, TQ, C_W), lambda bi, qi: (bi, qi, Q_QC // C_W)),
                  pl.BlockSpec((None, TQ, C_W), lambda bi, qi: (bi, qi, G_GC // C_W)),
                  pl.BlockSpec((NGT, TQ), lambda bi, qi: (0, bi * nq + qi)),
                  pl.BlockSpec((None, ncmp, LANES), lambda bi, qi: (bi, 0, 0)),
                  pl.BlockSpec((None, LANES, ncmp), lambda bi, qi: (bi, 0, 0)),
                  krow(Q_KS), vrow(T_VS), krow(Q_KW), vrow(T_VW),
                  pl.BlockSpec(ovt.shape, lambda bi, qi: (0, 0)),
                  pl.BlockSpec(snd.shape, lambda bi, qi: (0, 0, 0))],
        out_specs=pl.BlockSpec((None, TQ, C_W), lambda bi, qi: (bi, qi, 0)),
        out_shape=jax.ShapeDtypeStruct((b, s, C_W), BF16),
        scratch_shapes=[pltpu.VMEM((C_HEADS * TQ, KV_C), BF16),
                        pltpu.VMEM((2, C_KV, V_ROWS, s), BF16),
                        pltpu.VMEM((TK, C_HEADS * TQ), BF16),
                        pltpu.VMEM((2, C_HEADS // 2, V_ROWS, 2 * TQ), F32),
                        pltpu.VMEM((C_KV, n_slc, TQ), F32),
                        pltpu.VMEM((2, C_HEADS, TQ), F32),
                        pltpu.VMEM((AHEAD, TK, 2 * TQ), F32)],
        compiler_params=_params("arbitrary", "arbitrary"),
        name="mixer_c",
    )(hq3, hg3, gt, kcmp, vcmpt, hq3, vt, hq3, vt, ovt, snd)


def kernel(x, norm_w, w_in, w_out, diff_lq1, diff_lk1, diff_lq2, diff_lk2, diff_subln, sinks,
           cmp_pe_k, cmp_pe_v, cmp_wk1, cmp_wk2, cmp_wv1, cmp_wv2, final_norm):
    b, s, d = x.shape
    depth = w_in.shape[0]
    assert s % TK == 0 and (b * s) % TM == 0 and d % LANES == 0

    w_rm, w_t = _prep_w_in(w_in)
    w_out_p = _prep_w_out(w_out)
    pe2 = lambda pe: jnp.concatenate([pe, pe], axis=-1)
    w1 = lambda w: jnp.concatenate([w.reshape(depth, CMP_LEN, HEAD_DIM, CMP_HID)] * 2, axis=2).astype(BF16)
    zero2 = jnp.zeros((depth, CMP_HID, HEAD_DIM), F32)
    w2 = lambda w: jnp.stack([jnp.concatenate([w, zero2], axis=-1),
                              jnp.concatenate([zero2, w], axis=-1)], axis=1)
    pek, pev = pe2(cmp_pe_k), pe2(cmp_pe_v)
    wk1, wv1 = w1(cmp_wk1), w1(cmp_wv1)
    wk2 = w2(cmp_wk2).astype(BF16)
    wv2t = jnp.swapaxes(w2(cmp_wv2), 2, 3).astype(BF16)
    sub = jnp.broadcast_to(diff_subln[:, :, None], (depth, HEAD_DIM, TQ))
    fw = final_norm[None, :]
    x2 = x.reshape(b * s, d)
    for l in range(depth):
        hq, hg, vt, gt = _in_proj(x2, norm_w[l][None, :], w_rm[l], w_t[l])
        hq3 = hq.reshape(b, s, NQ)
        hg3 = hg.reshape(b, s, NG)
        kcmp, vcmpt = _compress(hg3, pek[l], pev[l], wk1[l], wk2[l], wv1[l], wv2t[l])
        lam_init = 0.8 - 0.6 * math.exp(-0.3 * l)
        ya = _mixer_a(hq3, hg3, vt, diff_lq1[l][None, :], diff_lk1[l][None, :], diff_lq2[l][None, :],
                      diff_lk2[l][None, :], sub[l], lam_init)
        yb = _mixer_b(sinks[l], hq3, hg3, vt)
        yc = _mixer_c(hq3, hg3, vt, gt, kcmp, vcmpt)
        x2 = _out_proj(ya.reshape(b * s, A_W), yb.reshape(b * s, B_W), yc.reshape(b * s, C_W),
                       w_out_p[l], x2, fw, l == depth - 1)
    return x2.reshape(b, s, d)
```

```python
import functools
import math

import numpy as np
import jax
import jax.numpy as jnp
from jax import lax
from jax.experimental import pallas as pl
from jax.experimental.pallas import tpu as pltpu

F32 = jnp.float32
BF16 = jnp.bfloat16

HEAD_DIM = 64
RMS_EPS = 1e-6
NEG = -1e30
FORCE = 1e30

A_HEADS = 4
A_QK = HEAD_DIM // 2
B_HEADS = 4
B_KV = 2
B_WINDOW = 128
C_HEADS = 8
C_KV = 2
CMP_LEN = 32
CMP_STRIDE = 16
CMP_HID = 128
SLC_LEN = 64
SLC_TOPN = 8
C_WINDOW = 512

A_W = A_HEADS * HEAD_DIM
B_W = B_HEADS * HEAD_DIM
C_W = C_HEADS * HEAD_DIM
D_MIX = A_W + B_W + C_W
KV_B = B_KV * HEAD_DIM
KV_C = C_KV * HEAD_DIM
N_GATES = 3 * C_HEADS

LANES = 128
SUBLANES = 8
BF16_ROWS = 16
TQ = 128
TK = 256
TM = 512
AHEAD = 4
V_ROWS = HEAD_DIM + BF16_ROWS
VMEM_LIMIT = 48 * 1024 * 1024

LOG2E = math.log2(math.e)

_O = dict(qa=0, ka=256, va=512, ga=768, qb=1024, kb=1280, vb=1408, gb=1536, qc=1792,
          kc=2304, vc=2432, ks=2560, vs=2688, kw=2816, vw=2944, gl=3072, gc=3096)

K_KA, K_KB, K_KS, K_KW = 0, 256, 384, 512
NK = 640
G_GA, G_GB, G_GC, G_KC, G_VC = 0, 256, 512, 1024, 1152
NG = 1280
T_QA, T_QB, T_QC, T_VA, T_VB, T_VS, T_VW = 0, 256, 512, 1024, 1280, 1408, 1536
NT = 1664
NGT = 32


def _slopes(n):
    return [float(2.0 ** (-8.0 * h / n)) * LOG2E for h in range(1, n + 1)]


_SLOPE_A = _slopes(A_HEADS)
_SLOPE_B = _slopes(B_HEADS)
_SLOPE_C = _slopes(C_HEADS)


def _params(*sem):
    return pltpu.CompilerParams(dimension_semantics=sem, vmem_limit_bytes=VMEM_LIMIT)


def _dot_nt(a, b):
    return lax.dot_general(a, b, (((1,), (1,)), ((), ())), preferred_element_type=F32)


def _dot(a, b):
    return jnp.dot(a, b, preferred_element_type=F32)


def _silu(g):
    return g * (1.0 / (1.0 + jnp.exp(-g)))


def _prep_w_in(w_in):
    depth, d, _ = w_in.shape
    col = lambda name, width: w_in[:, :, _O[name]:_O[name] + width]
    w_rm = jnp.concatenate([col('ka', A_W), col('kb', KV_B), col('ks', KV_C), col('kw', KV_C),
                            col('ga', A_W), col('gb', B_W), col('gc', C_W), col('kc', KV_C), col('vc', KV_C)],
                           axis=2).astype(BF16)
    gl = jnp.swapaxes(col('gl', N_GATES).reshape(depth, d, C_HEADS, 3), 2, 3).reshape(depth, d, N_GATES)
    gl = jnp.concatenate([gl, jnp.zeros((depth, d, NGT - N_GATES), w_in.dtype)], axis=2)
    cols_t = jnp.concatenate([col('qa', A_W) * (A_QK ** -0.5 * LOG2E),
                              col('qb', B_W) * (HEAD_DIM ** -0.5 * LOG2E),
                              col('qc', C_W) * (HEAD_DIM ** -0.5 * LOG2E),
                              col('va', A_W), col('vb', KV_B), col('vs', KV_C), col('vw', KV_C), gl], axis=2)
    w_t = jnp.swapaxes(cols_t, 1, 2).astype(BF16)
    return w_rm, w_t


def _snd_tiles(slopes):
    nd = np.arange(TK)[:, None] - np.arange(TQ)[None, :]
    return jnp.asarray(np.stack([s * nd for s in slopes]).astype(np.float32))


def _in_proj_kernel(x_ref, nw_ref, w_ref, wt_ref, hk_ref, hg_ref, ht_ref, gt_ref):
    x = x_ref[...]
    ms = jnp.mean(x * x, axis=-1, keepdims=True)
    xn = ((x * lax.rsqrt(ms + RMS_EPS)) * nw_ref[...]).astype(BF16)
    step = 256
    for c0 in range(0, NK, step):
        c1 = min(c0 + step, NK)
        hk_ref[:, c0:c1] = _dot(xn, w_ref[:, c0:c1]).astype(BF16)
    for c0 in range(0, NG, step):
        c1 = min(c0 + step, NG)
        hg_ref[:, c0:c1] = _dot(xn, w_ref[:, NK + c0:NK + c1])
    for r0 in range(0, NT, step):
        r1 = min(r0 + step, NT)
        ht_ref[r0:r1, :] = _dot_nt(wt_ref[r0:r1, :], xn).astype(BF16)
    gt_ref[...] = _dot_nt(wt_ref[NT:NT + NGT, :], xn)


def _in_proj(x2, nw, w_rm, w_t):
    n, d = x2.shape
    return pl.pallas_call(
        _in_proj_kernel,
        grid=(n // TM,),
        in_specs=[pl.BlockSpec((TM, d), lambda i: (i, 0)),
                  pl.BlockSpec((1, d), lambda i: (0, 0)),
                  pl.BlockSpec((d, NK + NG), lambda i: (0, 0)),
                  pl.BlockSpec((NT + NGT, d), lambda i: (0, 0))],
        out_specs=[pl.BlockSpec((TM, NK), lambda i: (i, 0)),
                   pl.BlockSpec((TM, NG), lambda i: (i, 0)),
                   pl.BlockSpec((NT, TM), lambda i: (0, i)),
                   pl.BlockSpec((NGT, TM), lambda i: (0, i))],
        out_shape=[jax.ShapeDtypeStruct((n, NK), BF16), jax.ShapeDtypeStruct((n, NG), F32),
                   jax.ShapeDtypeStruct((NT, n), BF16), jax.ShapeDtypeStruct((NGT, n), F32)],
        compiler_params=_params("arbitrary"),
        name="in_proj",
    )(x2, nw, w_rm, w_t)


def _out_proj_kernel(y_ref, w_ref, x_ref, fw_ref, o_ref, *, final):
    acc = x_ref[...] + _dot(y_ref[...], w_ref[...])
    if final:
        ms = jnp.mean(acc * acc, axis=-1, keepdims=True)
        acc = (acc * lax.rsqrt(ms + RMS_EPS)) * fw_ref[...]
    o_ref[...] = acc


def _out_proj(y, w, x2, fw, final):
    n, d = x2.shape
    return pl.pallas_call(
        functools.partial(_out_proj_kernel, final=final),
        grid=(n // TM,),
        in_specs=[pl.BlockSpec((TM, D_MIX), lambda i: (i, 0)),
                  pl.BlockSpec((D_MIX, d), lambda i: (0, 0)),
                  pl.BlockSpec((TM, d), lambda i: (i, 0)),
                  pl.BlockSpec((1, d), lambda i: (0, 0))],
        out_specs=pl.BlockSpec((TM, d), lambda i: (i, 0)),
        out_shape=jax.ShapeDtypeStruct((n, d), F32),
        compiler_params=_params("arbitrary"),
        name="out_proj",
    )(y, w, x2, fw)


def _compress_kernel(kc_ref, vc_ref, pek_ref, pev_ref, wk1_ref, wk2_ref, wv1_ref, wv2_ref,
                     kcmp_ref, vcmpt_ref):
    nch = kc_ref.shape[0] // CMP_STRIDE
    left = lax.broadcasted_iota(jnp.int32, (nch, LANES), 1) < HEAD_DIM

    def one(x_ref, pe_ref, w1_ref, w2_ref, out_ref, transposed):
        u = [jnp.zeros((nch, CMP_HID), F32) for _ in range(C_KV)]
        v = [jnp.zeros((nch, CMP_HID), F32) for _ in range(C_KV)]
        for l in range(CMP_STRIDE):
            xl = x_ref[pl.ds(l, nch, stride=CMP_STRIDE), :]
            pe_u = pe_ref[l:l + 1, :]
            pe_v = pe_ref[CMP_STRIDE + l:CMP_STRIDE + l + 1, :]
            w_u = w1_ref[l]
            w_v = w1_ref[CMP_STRIDE + l]
            for g in range(C_KV):
                keep = left if g == 0 else jnp.logical_not(left)
                u[g] = u[g] + _dot(jnp.where(keep, xl + pe_u, 0.0).astype(BF16), w_u)
                v[g] = v[g] + _dot(jnp.where(keep, xl + pe_v, 0.0).astype(BF16), w_v)
        acc = jnp.zeros((LANES, nch) if transposed else (nch, LANES), F32)
        for g in range(C_KV):
            hid = u[g] + pltpu.roll(v[g], nch - 1, 0)
            act = _silu(hid).astype(BF16)
            acc = acc + (_dot_nt(w2_ref[g], act) if transposed else _dot(act, w2_ref[g]))
        out_ref[...] = acc.astype(BF16)

    one(kc_ref, pek_ref, wk1_ref, wk2_ref, kcmp_ref, False)
    one(vc_ref, pev_ref, wv1_ref, wv2_ref, vcmpt_ref, True)


def _compress(hg3, pek, pev, wk1, wk2, wv1, wv2t):
    b, s, _ = hg3.shape
    nch = s // CMP_STRIDE
    full = lambda a: pl.BlockSpec(a.shape, lambda i: (0,) * a.ndim)
    return pl.pallas_call(
        _compress_kernel,
        grid=(b,),
        in_specs=[pl.BlockSpec((None, s, LANES), lambda i: (i, 0, G_KC // LANES)),
                  pl.BlockSpec((None, s, LANES), lambda i: (i, 0, G_VC // LANES)),
                  full(pek), full(pev), full(wk1), full(wk2), full(wv1), full(wv2t)],
        out_specs=[pl.BlockSpec((None, nch, LANES), lambda i: (i, 0, 0)),
                   pl.BlockSpec((None, LANES, nch), lambda i: (i, 0, 0))],
        out_shape=[jax.ShapeDtypeStruct((b, nch, LANES), BF16), jax.ShapeDtypeStruct((b, LANES, nch), BF16)],
        compiler_params=_params("arbitrary"),
        name="compress",
    )(hg3, hg3, pek, pev, wk1, wk2, wv1, wv2t)


def _fill_value_rows(vaug_ref, vt_ref, n_heads):
    s = vt_ref.shape[1]
    for h in range(n_heads):
        vaug_ref[h, 0:HEAD_DIM, :] = vt_ref[h * HEAD_DIM:(h + 1) * HEAD_DIM, :]
        vaug_ref[h, HEAD_DIM:V_ROWS, :] = jnp.ones((BF16_ROWS, s), BF16)


def _stack_gqa_q(qt_ref, qst_ref, heads_per_group):
    zero = jnp.zeros((HEAD_DIM, TQ), BF16)
    for p in range(qst_ref.shape[0]):
        cols = []
        for h in (2 * p, 2 * p + 1):
            q = qt_ref[h * HEAD_DIM:(h + 1) * HEAD_DIM, :]
            cols.append(jnp.concatenate([q, zero] if h // heads_per_group == 0 else [zero, q], axis=0))
        qst_ref[p] = jnp.concatenate(cols, axis=1)


def _key_minus_query():
    r = lax.broadcasted_iota(jnp.int32, (TK, TQ), 0)
    c = lax.broadcasted_iota(jnp.int32, (TK, TQ), 1)
    return (r - c).astype(F32)


def _flash_pair(st, c0, snd, slopes, kappa, madd, m_ref, e_ref):
    alphas = []
    for j in range(2):
        c = c0 + j
        u = st[:, j * TQ:(j + 1) * TQ] + snd(c)
        if madd is not None:
            u = u + madd
        sigma = slopes[c] * kappa
        m_old = m_ref[c:c + 1, :]
        m_new = jnp.maximum(m_old, jnp.max(u, axis=0, keepdims=True) + sigma)
        e_ref[:, c * TQ:(c + 1) * TQ] = jnp.exp2(u - (m_new - sigma)).astype(BF16)
        alphas.append(jnp.exp2(m_old - m_new))
        m_ref[c:c + 1, :] = m_new
    return alphas


def _accumulate(acc_ref, idx, vaug_chunk, e_ref, c0, alphas):
    a = jnp.concatenate(alphas, axis=1)
    acc_ref[idx] = acc_ref[idx] * a + _dot(vaug_chunk, e_ref[:, c0 * TQ:(c0 + 2) * TQ])


def _pipelined(items, st_ref=None, lookahead=()):
    if st_ref is None:
        queue = [scores() for scores, _ in items[:AHEAD]]
    else:
        assert len(items) >= AHEAD and len(lookahead) in (0, AHEAD)
        queue = [functools.partial(lambda j: st_ref[j], j) for j in range(AHEAD)]
    todo = [scores for scores, _ in items[AHEAD:]] + [scores for scores, _ in lookahead]
    for _, consume in items:
        if todo:
            queue.append(todo.pop(0)())
        st = queue.pop(0)
        consume(st() if callable(st) else st)
    for j, st in enumerate(queue):
        st_ref[j] = st


def _normalized(acc, lane0):
    return acc[0:HEAD_DIM, lane0:lane0 + TQ] / acc[HEAD_DIM:HEAD_DIM + 1, lane0:lane0 + TQ]


def _split3(x):
    hi = x.astype(BF16)
    r1 = x - hi.astype(F32)
    mid = r1.astype(BF16)
    lo = (r1 - mid.astype(F32)).astype(BF16)
    return hi, mid, lo


def _mixers_kernel(sink_ref,
                   qa_ref, qb_ref, qc_ref, ka_ref, kb_ref, ks_ref, kw_ref, va_ref, vb_ref, vs_ref, vw_ref,
                   kcmp_ref, vcmpt_ref, ga_ref, gb_ref, gc_ref, glt_ref,
                   lq1_ref, lk1_ref, lq2_ref, lk2_ref, sub_ref, ovt_ref, snda_ref, sndb_ref, sndc_ref,
                   y_ref,
                   qsta_ref, qstb_ref, qstc_ref, vauga_ref, vaugb_ref, vaugc_ref,
                   ea_ref, eb_ref, ec_ref, acca_ref, accb_ref, accc_ref, ma_ref, mb_ref, mc_ref,
                   msel_ref, st_ref, *, lam_init):
    qi = pl.program_id(1)
    s_len = ka_ref.shape[0]
    n_slc = s_len // SLC_LEN
    ncmp = kcmp_ref.shape[0]
    blocks_per_chunk = TK // SLC_LEN
    cpg = C_HEADS // C_KV

    @pl.when(qi == 0)
    def _():
        _fill_value_rows(vauga_ref, va_ref, A_HEADS)
        _fill_value_rows(vaugb_ref, vb_ref, B_KV)
        _fill_value_rows(vaugc_ref.at[0], vs_ref, C_KV)
        _fill_value_rows(vaugc_ref.at[1], vw_ref, C_KV)

    sub32 = lax.broadcasted_iota(jnp.int32, (LANES, TQ), 0) // A_QK
    for h in range(A_HEADS):
        blk = h * HEAD_DIM // LANES
        q = qa_ref[blk * LANES:(blk + 1) * LANES, :]
        zero = jnp.zeros_like(q)
        first = 2 * (h % (LANES // HEAD_DIM))
        qsta_ref[h] = jnp.concatenate([jnp.where(sub32 == first, q, zero), jnp.where(sub32 == first + 1, q, zero)],
                                      axis=1)
    _stack_gqa_q(qb_ref, qstb_ref, B_HEADS // B_KV)
    _stack_gqa_q(qc_ref, qstc_ref, cpg)
    acca_ref[...] = jnp.zeros(acca_ref.shape, F32)
    accc_ref[...] = jnp.zeros(accc_ref.shape, F32)
    den_rows = lax.broadcasted_iota(jnp.int32, (V_ROWS, 2 * TQ), 0) >= HEAD_DIM
    for g in range(B_KV):
        accb_ref[g] = jnp.where(den_rows, 1.0, 0.0)
    for c in range(B_HEADS):
        mb_ref[c:c + 1, :] = jnp.full((1, TQ), sink_ref[c] * LOG2E, F32)
    ma_ref[...] = jnp.full(ma_ref.shape, NEG, F32)
    mc_ref[...] = jnp.full(mc_ref.shape, NEG, F32)
    t0 = qi * TQ
    kmq = _key_minus_query()

    def start(ch):
        return pl.multiple_of(ch * TK, TK)

    def item_a(ch, diagonal, h):
        blk = h * HEAD_DIM // LANES

        def scores():
            return _dot(ka_ref[pl.ds(start(ch), TK), blk * LANES:(blk + 1) * LANES], qsta_ref[h])

        def consume(st):
            s0 = start(ch)
            kappa = (s0 - t0).astype(F32)
            madd = jnp.where(kmq + kappa <= 0.0, 0.0, NEG) if diagonal else None
            alphas = _flash_pair(st, 2 * h, lambda c: snda_ref[c // 2], [_SLOPE_A[c // 2] for c in range(2 * A_HEADS)],
                                 kappa, madd, ma_ref, ea_ref)
            _accumulate(acca_ref, h, vauga_ref[h, :, pl.ds(s0, TK)], ea_ref, 2 * h, alphas)

        return scores, consume

    masks = {}

    def mask_c(tag, br, ch, extra, g):
        key = (tag, g if br == 0 else 0)
        if key not in masks:
            nd = kmq + (start(ch) - t0).astype(F32)
            if br == 0:
                rows = [jnp.broadcast_to(msel_ref[g, pl.ds(ch * blocks_per_chunk + j, 1), :], (SLC_LEN, TQ))
                        for j in range(blocks_per_chunk)]
                madd = jnp.concatenate(rows, axis=0)
                if extra:
                    madd = madd + jnp.where(nd <= 0.0, 0.0, NEG)
            else:
                madd = jnp.where(jnp.logical_and(nd <= 0.0, nd > extra), 0.0, NEG)
            masks[key] = madd
        return masks[key]

    def item_c(tag, br, ch, extra, p):
        k_ref = ks_ref if br == 0 else kw_ref
        g = 2 * p // cpg

        def scores():
            return _dot(k_ref[pl.ds(start(ch), TK), :], qstc_ref[p])

        def consume(st):
            s0 = start(ch)
            kappa = (s0 - t0).astype(F32)
            alphas = _flash_pair(st, 2 * p, lambda c: sndc_ref[c], _SLOPE_C, kappa, mask_c(tag, br, ch, extra, g),
                                 mc_ref.at[br], ec_ref)
            _accumulate(accc_ref.at[br], p, vaugc_ref[br, g, :, pl.ds(s0, TK)], ec_ref, 2 * p, alphas)

        return scores, consume

    def item_b(g):
        s0 = pl.multiple_of(jnp.maximum(t0 - B_WINDOW, 0), LANES)

        def scores():
            return _dot(kb_ref[pl.ds(s0, TK), :], qstb_ref[g])

        def consume(st):
            kappa = (s0 - t0).astype(F32)
            nd = kmq + kappa
            madd = jnp.where(jnp.logical_and(nd <= 0.0, nd > -float(B_WINDOW)), 0.0, NEG)
            alphas = _flash_pair(st, 2 * g, lambda c: sndb_ref[c], _SLOPE_B, kappa, madd, mb_ref, eb_ref)
            _accumulate(accb_ref, g, vaugb_ref[g, :, pl.ds(s0, TK)], eb_ref, 2 * g, alphas)

        return scores, consume

    def causal_items(tag, ch, diagonal):
        return ([item_a(ch, diagonal, h) for h in range(A_HEADS)]
                + [item_c(tag, 0, ch, diagonal, p) for p in range(C_HEADS // 2)])

    last = (t0 + TQ - 1) // TK
    for j, (scores, _) in enumerate(causal_items('first', 0, False)[:AHEAD]):
        st_ref[j] = scores()

    p_cmp = []
    tpos = t0 + lax.broadcasted_iota(jnp.int32, (ncmp, TQ), 1)
    cend = lax.broadcasted_iota(jnp.int32, (ncmp, TQ), 0) * CMP_STRIDE + (CMP_LEN - 1)
    dist_c = (tpos - cend).astype(F32)
    valid_c = dist_c >= 0.0
    for p in range(C_HEADS // 2):
        st = _dot(kcmp_ref[...], qstc_ref[p])
        for j in range(2):
            c = 2 * p + j
            sc = jnp.where(valid_c, st[:, j * TQ:(j + 1) * TQ] - _SLOPE_C[c] * dist_c, NEG)
            mx = jnp.max(sc, axis=0, keepdims=True)
            e = jnp.where(valid_c, jnp.exp2(sc - mx), 0.0)
            den = jnp.sum(e, axis=0, keepdims=True)
            p_cmp.append(e / jnp.where(den > 0.0, den, 1.0))
    o_cmp = [_dot(vcmpt_ref[g * HEAD_DIM:(g + 1) * HEAD_DIM, :],
                  jnp.concatenate(p_cmp[g * cpg:(g + 1) * cpg], axis=1).astype(BF16))
             for g in range(C_KV)]

    jidx = lax.broadcasted_iota(jnp.int32, (n_slc, TQ), 0)
    tl = t0 + lax.broadcasted_iota(jnp.int32, (n_slc, TQ), 1)
    cur = tl // SLC_LEN
    causal_b = jidx * SLC_LEN <= tl
    forced = jnp.logical_or(jidx == 0, jnp.logical_or(jidx == cur, jidx == cur - 1))
    for g in range(C_KV):
        psum = p_cmp[g * cpg]
        for i in range(1, cpg):
            psum = psum + p_cmp[g * cpg + i]
        imp = jnp.zeros((n_slc, TQ), F32)
        for part in _split3(psum):
            imp = imp + _dot(ovt_ref[...], part)
        score = jnp.where(causal_b, jnp.where(forced, FORCE, imp), NEG)
        rank = jnp.zeros((n_slc, TQ), F32)
        for jp in range(n_slc):
            row = score[jp:jp + 1, :]
            tie = jnp.where(jidx > jp, 1.0, 0.0)
            rank = rank + jnp.where(row > score, 1.0, jnp.where(row == score, tie, 0.0))
        msel_ref[g] = jnp.where(rank < float(SLC_TOPN), 0.0, NEG)

    def two_chunks(i, carry):
        masks.clear()
        _pipelined(causal_items(0, 2 * i, False) + causal_items(1, 2 * i + 1, False), st_ref,
                   causal_items(2, 2 * i + 2, False)[:AHEAD])
        masks.clear()
        return carry

    lax.fori_loop(0, last // 2, two_chunks, 0)

    n_win = max((t + TQ - 1) // TK - (t - C_WINDOW + 1) // TK + 1 for t in range(0, TK * TQ, TQ))

    def tail_items():
        items = []
        for r in reversed(range(n_win)):
            ch = jnp.maximum(last - r, 0)
            lower = jnp.where(last - r >= 0, -float(C_WINDOW), -NEG)
            items += [item_c(('win', r), 1, ch, lower, p) for p in range(C_HEADS // 2)]
        return items + [item_b(g) for g in range(B_KV)]

    @pl.when(last % 2 == 1)
    def _():
        masks.clear()
        _pipelined(causal_items(0, last - 1, False) + causal_items(1, last, True) + tail_items(), st_ref)
        masks.clear()

    @pl.when(last % 2 == 0)
    def _():
        masks.clear()
        _pipelined(causal_items(1, last, True) + tail_items(), st_ref)
        masks.clear()

    def emit(col0, outs, g_ref):
        for i in range(len(outs) // 2):
            o = jnp.transpose(jnp.concatenate(outs[2 * i:2 * i + 2], axis=0))
            gate = _silu(g_ref[:, i * LANES:(i + 1) * LANES])
            y_ref[:, col0 + i * LANES:col0 + (i + 1) * LANES] = (o * gate).astype(BF16)

    lam = (jnp.exp(jnp.sum(lq1_ref[...] * lk1_ref[...], axis=1, keepdims=True))
           - jnp.exp(jnp.sum(lq2_ref[...] * lk2_ref[...], axis=1, keepdims=True)) + lam_init)
    outs = []
    for h in range(A_HEADS):
        acc = acca_ref[h]
        d = _normalized(acc, 0) - lam * _normalized(acc, TQ)
        ms = jnp.mean(d * d, axis=0, keepdims=True)
        outs.append(((d * lax.rsqrt(ms + RMS_EPS)) * sub_ref[...]) * (1.0 - lam_init))
    emit(0, outs, ga_ref)

    outs = []
    for c in range(B_HEADS):
        outs.append(_normalized(accb_ref[c // 2], (c % 2) * TQ))
    emit(A_W, outs, gb_ref)

    sig = 1.0 / (1.0 + jnp.exp(-glt_ref[...]))
    outs = []
    for c in range(C_HEADS):
        g, i = c // cpg, c % cpg
        gate = lambda br: sig[br * C_HEADS + c:br * C_HEADS + c + 1, :]
        lane0 = (c % 2) * TQ
        outs.append(gate(0) * o_cmp[g][:, i * TQ:(i + 1) * TQ]
                    + gate(1) * _normalized(accc_ref[0, c // 2], lane0)
                    + gate(2) * _normalized(accc_ref[1, c // 2], lane0))
    emit(A_W + B_W, outs, gc_ref)


def _overlap_t(ncmp_pad, n_slc):
    cs = np.arange(ncmp_pad)[None, :] * CMP_STRIDE
    ss = np.arange(n_slc)[:, None] * SLC_LEN
    ov = np.minimum(cs + CMP_LEN, ss + SLC_LEN) - np.maximum(cs, ss)
    ov = np.maximum(ov, 0) / CMP_LEN
    ov[:, ncmp_pad - 1] = 0.0
    return ov.astype(np.float32)


def _mixers(sinks, hk3, hg3, ht, gt, kcmp, vcmpt, lq1, lk1, lq2, lk2, sub, lam_init):
    b, s, _ = hk3.shape
    assert B_WINDOW + TQ <= TK and B_HEADS <= SUBLANES
    n_slc = s // SLC_LEN
    ncmp = kcmp.shape[1]
    nq = s // TQ
    ovt = jnp.asarray(_overlap_t(ncmp, n_slc), dtype=BF16)
    snda = _snd_tiles(_SLOPE_A)
    sndb = _snd_tiles(_SLOPE_B)
    sndc = _snd_tiles(_SLOPE_C)
    const = lambda a: pl.BlockSpec(a.shape, lambda bi, qi: (0,) * a.ndim)
    qt = lambda off, rows: pl.BlockSpec((rows, TQ), lambda bi, qi: (off // rows, bi * nq + qi))
    vt = lambda off, rows: pl.BlockSpec((rows, s), lambda bi, qi: (off // rows, bi))
    krow = lambda off, cols: pl.BlockSpec((None, s, cols), lambda bi, qi: (bi, 0, off // cols))
    gate = lambda off, cols: pl.BlockSpec((None, TQ, cols), lambda bi, qi: (bi, qi, off // cols))
    per_batch = lambda a: pl.BlockSpec((None,) + a.shape[1:], lambda bi, qi: (bi, 0, 0))
    return pl.pallas_call(
        functools.partial(_mixers_kernel, lam_init=lam_init),
        grid=(b, nq),
        in_specs=[pl.BlockSpec(memory_space=pltpu.SMEM),
                  qt(T_QA, A_W), qt(T_QB, B_W), qt(T_QC, C_W),
                  krow(K_KA, A_W), krow(K_KB, KV_B), krow(K_KS, KV_C), krow(K_KW, KV_C),
                  vt(T_VA, A_W), vt(T_VB, KV_B), vt(T_VS, KV_C), vt(T_VW, KV_C),
                  per_batch(kcmp), per_batch(vcmpt),
                  gate(G_GA, A_W), gate(G_GB, B_W), gate(G_GC, C_W),
                  pl.BlockSpec((NGT, TQ), lambda bi, qi: (0, bi * nq + qi)),
                  const(lq1), const(lk1), const(lq2), const(lk2), const(sub), const(ovt),
                  const(snda), const(sndb), const(sndc)],
        out_specs=pl.BlockSpec((None, TQ, D_MIX), lambda bi, qi: (bi, qi, 0)),
        out_shape=jax.ShapeDtypeStruct((b, s, D_MIX), BF16),
        scratch_shapes=[pltpu.VMEM((A_HEADS, LANES, 2 * TQ), BF16),
                        pltpu.VMEM((B_HEADS // 2, LANES, 2 * TQ), BF16),
                        pltpu.VMEM((C_HEADS // 2, LANES, 2 * TQ), BF16),
                        pltpu.VMEM((A_HEADS, V_ROWS, s), BF16),
                        pltpu.VMEM((B_KV, V_ROWS, s), BF16),
                        pltpu.VMEM((2, C_KV, V_ROWS, s), BF16),
                        pltpu.VMEM((TK, 2 * A_HEADS * TQ), BF16),
                        pltpu.VMEM((TK, B_HEADS * TQ), BF16),
                        pltpu.VMEM((TK, C_HEADS * TQ), BF16),
                        pltpu.VMEM((A_HEADS, V_ROWS, 2 * TQ), F32),
                        pltpu.VMEM((B_HEADS // 2, V_ROWS, 2 * TQ), F32),
                        pltpu.VMEM((2, C_HEADS // 2, V_ROWS, 2 * TQ), F32),
                        pltpu.VMEM((2 * A_HEADS, TQ), F32),
                        pltpu.VMEM((SUBLANES, TQ), F32),
                        pltpu.VMEM((2, C_HEADS, TQ), F32),
                        pltpu.VMEM((C_KV, n_slc, TQ), F32),
                        pltpu.VMEM((AHEAD, TK, 2 * TQ), F32)],
        compiler_params=_params("arbitrary", "arbitrary"),
        name="mixers",
    )(sinks, ht, ht, ht, hk3, hk3, hk3, hk3, ht, ht, ht, ht, kcmp, vcmpt, hg3, hg3, hg3, gt,
      lq1, lk1, lq2, lk2, sub, ovt, snda, sndb, sndc)


def kernel(x, norm_w, w_in, w_out, diff_lq1, diff_lk1, diff_lq2, diff_lk2, diff_subln, sinks,
           cmp_pe_k, cmp_pe_v, cmp_wk1, cmp_wk2, cmp_wv1, cmp_wv2, final_norm):
    b, s, d = x.shape
    depth = w_in.shape[0]
    assert s % TK == 0 and s // TK >= 3 and (b * s) % TM == 0 and d % LANES == 0

    w_rm, w_t = _prep_w_in(w_in)
    w_out_b = w_out.astype(BF16)
    pe2 = lambda pe: jnp.concatenate([pe, pe], axis=-1)
    w1 = lambda w: jnp.concatenate([w.reshape(depth, CMP_LEN, HEAD_DIM, CMP_HID)] * 2, axis=2).astype(BF16)
    zero2 = jnp.zeros((depth, CMP_HID, HEAD_DIM), F32)
    w2 = lambda w: jnp.stack([jnp.concatenate([w, zero2], axis=-1),
                              jnp.concatenate([zero2, w], axis=-1)], axis=1)
    pek, pev = pe2(cmp_pe_k), pe2(cmp_pe_v)
    wk1, wv1 = w1(cmp_wk1), w1(cmp_wv1)
    wk2 = w2(cmp_wk2).astype(BF16)
    wv2t = jnp.swapaxes(w2(cmp_wv2), 2, 3).astype(BF16)
    sub = jnp.broadcast_to(diff_subln[:, :, None], (depth, HEAD_DIM, TQ))
    fw = final_norm[None, :]
    x2 = x.reshape(b * s, d)
    for l in range(depth):
        hk, hg, ht, gt = _in_proj(x2, norm_w[l][None, :], w_rm[l], w_t[l])
        hk3 = hk.reshape(b, s, NK)
        hg3 = hg.reshape(b, s, NG)
        kcmp, vcmpt = _compress(hg3, pek[l], pev[l], wk1[l], wk2[l], wv1[l], wv2t[l])
        lam_init = 0.8 - 0.6 * math.exp(-0.3 * l)
        y = _mixers(sinks[l], hk3, hg3, ht, gt, kcmp, vcmpt, diff_lq1[l][None, :], diff_lk1[l][None, :],
                    diff_lq2[l][None, :], diff_lk2[l][None, :], sub[l], lam_init)
        x2 = _out_proj(y.reshape(b * s, D_MIX), w_out_b[l], x2, fw, l == depth - 1)
    return x2.reshape(b, s, d)
```

```python
import functools
import math

import numpy as np
import jax
import jax.numpy as jnp
from jax import lax
from jax.experimental import pallas as pl
from jax.experimental.pallas import tpu as pltpu

F32 = jnp.float32
BF16 = jnp.bfloat16

HEAD_DIM = 64
RMS_EPS = 1e-6
NEG = -1e30
FORCE = 1e30

A_HEADS = 4
A_QK = HEAD_DIM // 2
B_HEADS = 4
B_KV = 2
B_WINDOW = 128
C_HEADS = 8
C_KV = 2
CMP_LEN = 32
CMP_STRIDE = 16
CMP_HID = 128
SLC_LEN = 64
SLC_TOPN = 8
C_WINDOW = 512

A_W = A_HEADS * HEAD_DIM
B_W = B_HEADS * HEAD_DIM
C_W = C_HEADS * HEAD_DIM
D_MIX = A_W + B_W + C_W
KV_B = B_KV * HEAD_DIM
KV_C = C_KV * HEAD_DIM
N_GATES = 3 * C_HEADS

LANES = 128
SUBLANES = 8
BF16_ROWS = 16
TQ = 128
TK = 256
TM = 512
AHEAD = 4
V_ROWS = HEAD_DIM + BF16_ROWS
VMEM_LIMIT = 48 * 1024 * 1024

LOG2E = math.log2(math.e)

_O = dict(qa=0, ka=256, va=512, ga=768, qb=1024, kb=1280, vb=1408, gb=1536, qc=1792,
          kc=2304, vc=2432, ks=2560, vs=2688, kw=2816, vw=2944, gl=3072, gc=3096)

K_KA, K_KB, K_KS, K_KW = 0, 256, 384, 512
NK = 640
G_GA, G_GB, G_GC, G_KC, G_VC = 0, 256, 512, 1024, 1152
NG = 1280
T_QA, T_QB, T_QC, T_VA, T_VB, T_VS, T_VW = 0, 256, 512, 1024, 1280, 1408, 1536
NT = 1664
NGT = 32


def _slopes(n):
    return [float(2.0 ** (-8.0 * h / n)) * LOG2E for h in range(1, n + 1)]


_SLOPE_A = _slopes(A_HEADS)
_SLOPE_B = _slopes(B_HEADS)
_SLOPE_C = _slopes(C_HEADS)


def _params(*sem):
    return pltpu.CompilerParams(dimension_semantics=sem, vmem_limit_bytes=VMEM_LIMIT)


def _dot_nt(a, b):
    return lax.dot_general(a, b, (((1,), (1,)), ((), ())), preferred_element_type=F32)


def _dot(a, b):
    return jnp.dot(a, b, preferred_element_type=F32)


def _silu(g):
    return g * (1.0 / (1.0 + jnp.exp(-g)))


def _prep_w_in(w_in):
    depth, d, _ = w_in.shape
    col = lambda name, width: w_in[:, :, _O[name]:_O[name] + width]
    w_rm = jnp.concatenate([col('ka', A_W), col('kb', KV_B), col('ks', KV_C), col('kw', KV_C),
                            col('ga', A_W), col('gb', B_W), col('gc', C_W), col('kc', KV_C), col('vc', KV_C)],
                           axis=2).astype(BF16)
    gl = jnp.swapaxes(col('gl', N_GATES).reshape(depth, d, C_HEADS, 3), 2, 3).reshape(depth, d, N_GATES)
    gl = jnp.concatenate([gl, jnp.zeros((depth, d, NGT - N_GATES), w_in.dtype)], axis=2)
    cols_t = jnp.concatenate([col('qa', A_W) * (A_QK ** -0.5 * LOG2E),
                              col('qb', B_W) * (HEAD_DIM ** -0.5 * LOG2E),
                              col('qc', C_W) * (HEAD_DIM ** -0.5 * LOG2E),
                              col('va', A_W), col('vb', KV_B), col('vs', KV_C), col('vw', KV_C), gl], axis=2)
    w_t = jnp.swapaxes(cols_t, 1, 2).astype(BF16)
    return w_rm, w_t


EXT_ROWS = 2 * BF16_ROWS
POS_SPLIT = 256


def _bf16_pieces(x, n=3):
    out, rest = [], np.float32(x)
    for _ in range(n):
        bits = np.array([rest], np.float32).view(np.uint32)
        bits = (bits + 0x7FFF + ((bits >> 16) & 1)) & np.uint32(0xFFFF0000)
        piece = bits.view(np.float32)[0]
        out.append(piece)
        rest = np.float32(rest - piece)
    return out


def _key_ext(s):
    pos = np.arange(s)
    ext = np.zeros((s, 2, LANES), np.float32)
    for j in range(3):
        ext[:, :, j] = (pos // POS_SPLIT)[:, None]
        ext[:, :, 3 + j] = (pos % POS_SPLIT)[:, None]
    blk = (pos % TK) // SLC_LEN
    ext[pos, 1, BF16_ROWS + blk] = 1.0
    return jnp.asarray(ext.reshape(s, 2 * LANES), dtype=BF16)


def _query_ext(slopes):
    ext = np.zeros((len(slopes) // 2, EXT_ROWS, 2, TQ), np.float32)
    for c, slope in enumerate(slopes):
        for j, piece in enumerate(_bf16_pieces(slope)):
            ext[c // 2, j, c % 2, :] = piece * POS_SPLIT
            ext[c // 2, 3 + j, c % 2, :] = piece
    return jnp.asarray(ext.reshape(len(slopes) // 2, EXT_ROWS, 2 * TQ), dtype=BF16)


def _in_proj_kernel(x_ref, nw_ref, w_ref, wt_ref, hk_ref, hg_ref, ht_ref, gt_ref):
    x = x_ref[...]
    ms = jnp.mean(x * x, axis=-1, keepdims=True)
    xn = ((x * lax.rsqrt(ms + RMS_EPS)) * nw_ref[...]).astype(BF16)
    step = 256
    for c0 in range(0, NK, step):
        c1 = min(c0 + step, NK)
        hk_ref[:, c0:c1] = _dot(xn, w_ref[:, c0:c1]).astype(BF16)
    for c0 in range(0, NG, step):
        c1 = min(c0 + step, NG)
        hg_ref[:, c0:c1] = _dot(xn, w_ref[:, NK + c0:NK + c1])
    for r0 in range(0, NT, step):
        r1 = min(r0 + step, NT)
        ht_ref[r0:r1, :] = _dot_nt(wt_ref[r0:r1, :], xn).astype(BF16)
    gt_ref[...] = _dot_nt(wt_ref[NT:NT + NGT, :], xn)


def _in_proj(x2, nw, w_rm, w_t):
    n, d = x2.shape
    return pl.pallas_call(
        _in_proj_kernel,
        grid=(n // TM,),
        in_specs=[pl.BlockSpec((TM, d), lambda i: (i, 0)),
                  pl.BlockSpec((1, d), lambda i: (0, 0)),
                  pl.BlockSpec((d, NK + NG), lambda i: (0, 0)),
                  pl.BlockSpec((NT + NGT, d), lambda i: (0, 0))],
        out_specs=[pl.BlockSpec((TM, NK), lambda i: (i, 0)),
                   pl.BlockSpec((TM, NG), lambda i: (i, 0)),
                   pl.BlockSpec((NT, TM), lambda i: (0, i)),
                   pl.BlockSpec((NGT, TM), lambda i: (0, i))],
        out_shape=[jax.ShapeDtypeStruct((n, NK), BF16), jax.ShapeDtypeStruct((n, NG), F32),
                   jax.ShapeDtypeStruct((NT, n), BF16), jax.ShapeDtypeStruct((NGT, n), F32)],
        compiler_params=_params("arbitrary"),
        name="in_proj",
    )(x2, nw, w_rm, w_t)


def _out_proj_kernel(y_ref, w_ref, x_ref, fw_ref, o_ref, *, final):
    acc = x_ref[...] + _dot(y_ref[...], w_ref[...])
    if final:
        ms = jnp.mean(acc * acc, axis=-1, keepdims=True)
        acc = (acc * lax.rsqrt(ms + RMS_EPS)) * fw_ref[...]
    o_ref[...] = acc


def _out_proj(y, w, x2, fw, final):
    n, d = x2.shape
    return pl.pallas_call(
        functools.partial(_out_proj_kernel, final=final),
        grid=(n // TM,),
        in_specs=[pl.BlockSpec((TM, D_MIX), lambda i: (i, 0)),
                  pl.BlockSpec((D_MIX, d), lambda i: (0, 0)),
                  pl.BlockSpec((TM, d), lambda i: (i, 0)),
                  pl.BlockSpec((1, d), lambda i: (0, 0))],
        out_specs=pl.BlockSpec((TM, d), lambda i: (i, 0)),
        out_shape=jax.ShapeDtypeStruct((n, d), F32),
        compiler_params=_params("arbitrary"),
        name="out_proj",
    )(y, w, x2, fw)


def _compress_kernel(kc_ref, vc_ref, pek_ref, pev_ref, wk1_ref, wk2_ref, wv1_ref, wv2_ref,
                     kcmp_ref, vcmpt_ref):
    nch = kc_ref.shape[0] // CMP_STRIDE
    left = lax.broadcasted_iota(jnp.int32, (nch, LANES), 1) < HEAD_DIM

    def one(x_ref, pe_ref, w1_ref, w2_ref, out_ref, transposed):
        u = [jnp.zeros((nch, CMP_HID), F32) for _ in range(C_KV)]
        v = [jnp.zeros((nch, CMP_HID), F32) for _ in range(C_KV)]
        for l in range(CMP_STRIDE):
            xl = x_ref[pl.ds(l, nch, stride=CMP_STRIDE), :]
            pe_u = pe_ref[l:l + 1, :]
            pe_v = pe_ref[CMP_STRIDE + l:CMP_STRIDE + l + 1, :]
            w_u = w1_ref[l]
            w_v = w1_ref[CMP_STRIDE + l]
            for g in range(C_KV):
                keep = left if g == 0 else jnp.logical_not(left)
                u[g] = u[g] + _dot(jnp.where(keep, xl + pe_u, 0.0).astype(BF16), w_u)
                v[g] = v[g] + _dot(jnp.where(keep, xl + pe_v, 0.0).astype(BF16), w_v)
        acc = jnp.zeros((LANES, nch) if transposed else (nch, LANES), F32)
        for g in range(C_KV):
            hid = u[g] + pltpu.roll(v[g], nch - 1, 0)
            act = _silu(hid).astype(BF16)
            acc = acc + (_dot_nt(w2_ref[g], act) if transposed else _dot(act, w2_ref[g]))
        out_ref[...] = acc.astype(BF16)

    one(kc_ref, pek_ref, wk1_ref, wk2_ref, kcmp_ref, False)
    one(vc_ref, pev_ref, wv1_ref, wv2_ref, vcmpt_ref, True)


def _compress(hg3, pek, pev, wk1, wk2, wv1, wv2t):
    b, s, _ = hg3.shape
    nch = s // CMP_STRIDE
    full = lambda a: pl.BlockSpec(a.shape, lambda i: (0,) * a.ndim)
    return pl.pallas_call(
        _compress_kernel,
        grid=(b,),
        in_specs=[pl.BlockSpec((None, s, LANES), lambda i: (i, 0, G_KC // LANES)),
                  pl.BlockSpec((None, s, LANES), lambda i: (i, 0, G_VC // LANES)),
                  full(pek), full(pev), full(wk1), full(wk2), full(wv1), full(wv2t)],
        out_specs=[pl.BlockSpec((None, nch, LANES), lambda i: (i, 0, 0)),
                   pl.BlockSpec((None, LANES, nch), lambda i: (i, 0, 0))],
        out_shape=[jax.ShapeDtypeStruct((b, nch, LANES), BF16), jax.ShapeDtypeStruct((b, LANES, nch), BF16)],
        compiler_params=_params("arbitrary"),
        name="compress",
    )(hg3, hg3, pek, pev, wk1, wk2, wv1, wv2t)


def _fill_value_rows(vaug_ref, vt_ref, n_heads):
    s = vt_ref.shape[1]
    for h in range(n_heads):
        vaug_ref[h, 0:HEAD_DIM, :] = vt_ref[h * HEAD_DIM:(h + 1) * HEAD_DIM, :]
        vaug_ref[h, HEAD_DIM:V_ROWS, :] = jnp.ones((BF16_ROWS, s), BF16)


def _stack_gqa_q(qt_ref, qst_ref, heads_per_group):
    zero = jnp.zeros((HEAD_DIM, TQ), BF16)
    for p in range(qst_ref.shape[0]):
        cols = []
        for h in (2 * p, 2 * p + 1):
            q = qt_ref[h * HEAD_DIM:(h + 1) * HEAD_DIM, :]
            cols.append(jnp.concatenate([q, zero] if h // heads_per_group == 0 else [zero, q], axis=0))
        qst_ref[p, 0:LANES, :] = jnp.concatenate(cols, axis=1)


def _set_ext_rows(qst_ref, qx_ref):
    n = qst_ref.shape[0]
    qst_ref[:, LANES:LANES + EXT_ROWS, :] = qx_ref[...]
    qst_ref[:, LANES + EXT_ROWS:, :] = jnp.zeros((n, LANES - EXT_ROWS, 2 * TQ), BF16)


def _ext_keys(k_chunk, kx_ref, s0, selected):
    c0 = LANES if selected else 0
    return jnp.concatenate([k_chunk, kx_ref[pl.ds(s0, TK), c0:c0 + LANES]], axis=1)


def _key_minus_query():
    r = lax.broadcasted_iota(jnp.int32, (TK, TQ), 0)
    c = lax.broadcasted_iota(jnp.int32, (TK, TQ), 1)
    return (r - c).astype(F32)


def _flash_pair(st, c0, madd, m_ref, e_ref):
    alphas = []
    for j in range(2):
        c = c0 + j
        u = st[:, j * TQ:(j + 1) * TQ]
        if madd is not None:
            u = u + madd
        m_old = m_ref[c:c + 1, :]
        m_new = jnp.maximum(m_old, jnp.max(u, axis=0, keepdims=True))
        e_ref[:, c * TQ:(c + 1) * TQ] = jnp.exp2(u - m_new).astype(BF16)
        alphas.append(jnp.exp2(m_old - m_new))
        m_ref[c:c + 1, :] = m_new
    return alphas


def _accumulate(acc_ref, idx, vaug_chunk, e_ref, c0, alphas):
    a = jnp.concatenate(alphas, axis=1)
    acc_ref[idx] = acc_ref[idx] * a + _dot(vaug_chunk, e_ref[:, c0 * TQ:(c0 + 2) * TQ])


def _pipelined(items, st_ref=None, lookahead=()):
    if st_ref is None:
        queue = [scores() for scores, _ in items[:AHEAD]]
    else:
        assert len(items) >= AHEAD and len(lookahead) in (0, AHEAD)
        queue = [functools.partial(lambda j: st_ref[j], j) for j in range(AHEAD)]
    todo = [scores for scores, _ in items[AHEAD:]] + [scores for scores, _ in lookahead]
    for _, consume in items:
        if todo:
            queue.append(todo.pop(0)())
        st = queue.pop(0)
        consume(st() if callable(st) else st)
    for j, st in enumerate(queue):
        st_ref[j] = st


def _normalized(acc, lane0):
    return acc[0:HEAD_DIM, lane0:lane0 + TQ] / acc[HEAD_DIM:HEAD_DIM + 1, lane0:lane0 + TQ]


def _split3(x):
    hi = x.astype(BF16)
    r1 = x - hi.astype(F32)
    mid = r1.astype(BF16)
    lo = (r1 - mid.astype(F32)).astype(BF16)
    return hi, mid, lo


def _mixers_kernel(sink_ref,
                   qa_ref, qb_ref, qc_ref, ka_ref, kb_ref, ks_ref, kw_ref, va_ref, vb_ref, vs_ref, vw_ref,
                   kcmp_ref, vcmpt_ref, ga_ref, gb_ref, gc_ref, glt_ref,
                   lq1_ref, lk1_ref, lq2_ref, lk2_ref, sub_ref, ovt_ref, kx_ref, qxa_ref, qxb_ref, qxc_ref,
                   y_ref,
                   qsta_ref, qstb_ref, qstc_ref, vauga_ref, vaugb_ref, vaugc_ref,
                   ea_ref, eb_ref, ec_ref, acca_ref, accb_ref, accc_ref, ma_ref, mb_ref, mc_ref,
                   msel_ref, st_ref, gsil_ref, *, lam_init):
    qi = pl.program_id(1)
    s_len = ka_ref.shape[0]
    n_slc = s_len // SLC_LEN
    ncmp = kcmp_ref.shape[0]
    blocks_per_chunk = TK // SLC_LEN
    cpg = C_HEADS // C_KV

    @pl.when(qi == 0)
    def _():
        _fill_value_rows(vauga_ref, va_ref, A_HEADS)
        _fill_value_rows(vaugb_ref, vb_ref, B_KV)
        _fill_value_rows(vaugc_ref.at[0], vs_ref, C_KV)
        _fill_value_rows(vaugc_ref.at[1], vw_ref, C_KV)

    sub32 = lax.broadcasted_iota(jnp.int32, (LANES, TQ), 0) // A_QK
    for h in range(A_HEADS):
        blk = h * HEAD_DIM // LANES
        q = qa_ref[blk * LANES:(blk + 1) * LANES, :]
        zero = jnp.zeros_like(q)
        first = 2 * (h % (LANES // HEAD_DIM))
        qsta_ref[h, 0:LANES, :] = jnp.concatenate(
            [jnp.where(sub32 == first, q, zero), jnp.where(sub32 == first + 1, q, zero)], axis=1)
    _stack_gqa_q(qb_ref, qstb_ref, B_HEADS // B_KV)
    _stack_gqa_q(qc_ref, qstc_ref, cpg)
    _set_ext_rows(qsta_ref, qxa_ref)
    _set_ext_rows(qstb_ref, qxb_ref)
    _set_ext_rows(qstc_ref, qxc_ref)
    acca_ref[...] = jnp.zeros(acca_ref.shape, F32)
    accc_ref[...] = jnp.zeros(accc_ref.shape, F32)
    t0 = qi * TQ
    den_rows = lax.broadcasted_iota(jnp.int32, (V_ROWS, 2 * TQ), 0) >= HEAD_DIM
    for g in range(B_KV):
        accb_ref[g] = jnp.where(den_rows, 1.0, 0.0)
    t_abs = (t0 + lax.broadcasted_iota(jnp.int32, (1, TQ), 1)).astype(F32)
    for c in range(B_HEADS):
        mb_ref[c:c + 1, :] = sink_ref[c] * LOG2E + _SLOPE_B[c] * t_abs
    ma_ref[...] = jnp.full(ma_ref.shape, NEG, F32)
    mc_ref[...] = jnp.full(mc_ref.shape, NEG, F32)
    kmq = _key_minus_query()

    def start(ch):
        return pl.multiple_of(ch * TK, TK)

    def item_a(tag, ch, diagonal, h):
        blk = h * HEAD_DIM // LANES

        def scores():
            s0 = start(ch)
            keys = _ext_keys(ka_ref[pl.ds(s0, TK), blk * LANES:(blk + 1) * LANES], kx_ref, s0, False)
            return _dot(keys, qsta_ref[h])

        def consume(st):
            s0 = start(ch)
            alphas = _flash_pair(st, 2 * h, mask_c(tag, 0, ch, diagonal), ma_ref, ea_ref)
            _accumulate(acca_ref, h, vauga_ref[h, :, pl.ds(s0, TK)], ea_ref, 2 * h, alphas)

        return scores, consume

    masks = {}

    def mask_c(tag, br, ch, extra):
        if br == 0 and not extra:
            return None
        if tag not in masks:
            nd = kmq + (start(ch) - t0).astype(F32)
            lower_ok = True if br == 0 else nd > extra
            masks[tag] = jnp.where(jnp.logical_and(nd <= 0.0, lower_ok), 0.0, NEG)
        return masks[tag]

    def item_c(tag, br, ch, extra, p):
        k_ref = ks_ref if br == 0 else kw_ref
        g = 2 * p // cpg

        def scores():
            s0 = start(ch)
            if br == 0:
                rows = msel_ref[g, ch]
                qstc_ref[p, LANES + BF16_ROWS:LANES + EXT_ROWS, :] = jnp.concatenate([rows, rows], axis=1)
            return _dot(_ext_keys(k_ref[pl.ds(s0, TK), :], kx_ref, s0, br == 0), qstc_ref[p])

        def consume(st):
            s0 = start(ch)
            alphas = _flash_pair(st, 2 * p, mask_c(tag, br, ch, extra), mc_ref.at[br], ec_ref)
            _accumulate(accc_ref.at[br], p, vaugc_ref[br, g, :, pl.ds(s0, TK)], ec_ref, 2 * p, alphas)

        return scores, consume

    def item_b(g):
        s0 = pl.multiple_of(jnp.maximum(t0 - B_WINDOW, 0), LANES)

        def scores():
            return _dot(_ext_keys(kb_ref[pl.ds(s0, TK), :], kx_ref, s0, False), qstb_ref[g])

        def consume(st):
            nd = kmq + (s0 - t0).astype(F32)
            madd = jnp.where(jnp.logical_and(nd <= 0.0, nd > -float(B_WINDOW)), 0.0, NEG)
            alphas = _flash_pair(st, 2 * g, madd, mb_ref, eb_ref)
            _accumulate(accb_ref, g, vaugb_ref[g, :, pl.ds(s0, TK)], eb_ref, 2 * g, alphas)

        return scores, consume

    def causal_items(tag, ch, diagonal):
        return ([item_a(tag, ch, diagonal, h) for h in range(A_HEADS)]
                + [item_c(tag, 0, ch, diagonal, p) for p in range(C_HEADS // 2)])

    last = (t0 + TQ - 1) // TK
    first = causal_items('first', 0, False)[:AHEAD]
    for j, (scores, _) in enumerate(first[:A_HEADS]):
        st_ref[j] = scores()
    for col0, g_ref in ((0, ga_ref), (A_W, gb_ref), (A_W + B_W, gc_ref)):
        gsil_ref[:, col0:col0 + g_ref.shape[1]] = _silu(g_ref[...])

    p_cmp = []
    tpos = t0 + lax.broadcasted_iota(jnp.int32, (ncmp, TQ), 1)
    cend = lax.broadcasted_iota(jnp.int32, (ncmp, TQ), 0) * CMP_STRIDE + (CMP_LEN - 1)
    dist_c = (tpos - cend).astype(F32)
    valid_c = dist_c >= 0.0
    for p in range(C_HEADS // 2):
        st = _dot(kcmp_ref[...], qstc_ref[p, 0:LANES, :])
        for j in range(2):
            c = 2 * p + j
            sc = jnp.where(valid_c, st[:, j * TQ:(j + 1) * TQ] - _SLOPE_C[c] * dist_c, NEG)
            mx = jnp.max(sc, axis=0, keepdims=True)
            e = jnp.where(valid_c, jnp.exp2(sc - mx), 0.0)
            den = jnp.sum(e, axis=0, keepdims=True)
            p_cmp.append(e / jnp.where(den > 0.0, den, 1.0))
    o_cmp = [_dot(vcmpt_ref[g * HEAD_DIM:(g + 1) * HEAD_DIM, :],
                  jnp.concatenate(p_cmp[g * cpg:(g + 1) * cpg], axis=1).astype(BF16))
             for g in range(C_KV)]

    jidx = lax.broadcasted_iota(jnp.int32, (n_slc, TQ), 0)
    tl = t0 + lax.broadcasted_iota(jnp.int32, (n_slc, TQ), 1)
    cur = tl // SLC_LEN
    causal_b = jidx * SLC_LEN <= tl
    forced = jnp.logical_or(jidx == 0, jnp.logical_or(jidx == cur, jidx == cur - 1))
    for g in range(C_KV):
        psum = p_cmp[g * cpg]
        for i in range(1, cpg):
            psum = psum + p_cmp[g * cpg + i]
        imp = jnp.zeros((n_slc, TQ), F32)
        for part in _split3(psum):
            imp = imp + _dot(ovt_ref[...], part)
        score = jnp.where(causal_b, jnp.where(forced, FORCE, imp), NEG)
        rank = jnp.zeros((n_slc, TQ), F32)
        for jp in range(n_slc):
            row = score[jp:jp + 1, :]
            tie = jnp.where(jidx > jp, 1.0, 0.0)
            rank = rank + jnp.where(row > score, 1.0, jnp.where(row == score, tie, 0.0))
        msel = jnp.where(rank < float(SLC_TOPN), 0.0, NEG)
        pad = jnp.zeros((BF16_ROWS - blocks_per_chunk, TQ), F32)
        for ch in range(n_slc // blocks_per_chunk):
            rows = msel[ch * blocks_per_chunk:(ch + 1) * blocks_per_chunk, :]
            msel_ref[g, ch] = jnp.concatenate([rows, pad], axis=0).astype(BF16)
    for j, (scores, _) in enumerate(first[A_HEADS:]):
        st_ref[A_HEADS + j] = scores()

    def two_chunks(i, carry):
        masks.clear()
        _pipelined(causal_items(0, 2 * i, False) + causal_items(1, 2 * i + 1, False), st_ref,
                   causal_items(2, 2 * i + 2, False)[:AHEAD])
        masks.clear()
        return carry

    lax.fori_loop(0, last // 2, two_chunks, 0)

    n_win = max((t + TQ - 1) // TK - (t - C_WINDOW + 1) // TK + 1 for t in range(0, TK * TQ, TQ))

    def tail_items():
        items = []
        for r in reversed(range(n_win)):
            ch = jnp.maximum(last - r, 0)
            lower = jnp.where(last - r >= 0, -float(C_WINDOW), -NEG)
            items += [item_c(('win', r), 1, ch, lower, p) for p in range(C_HEADS // 2)]
        return items + [item_b(g) for g in range(B_KV)]

    @pl.when(last % 2 == 1)
    def _():
        masks.clear()
        _pipelined(causal_items(0, last - 1, False) + causal_items(1, last, True) + tail_items(), st_ref)
        masks.clear()

    @pl.when(last % 2 == 0)
    def _():
        masks.clear()
        _pipelined(causal_items(1, last, True) + tail_items(), st_ref)
        masks.clear()

    def emit(col0, outs):
        for i in range(len(outs) // 2):
            o = jnp.transpose(jnp.concatenate(outs[2 * i:2 * i + 2], axis=0))
            cols = slice(col0 + i * LANES, col0 + (i + 1) * LANES)
            y_ref[:, cols] = (o * gsil_ref[:, cols]).astype(BF16)

    lam = (jnp.exp(jnp.sum(lq1_ref[...] * lk1_ref[...], axis=1, keepdims=True))
           - jnp.exp(jnp.sum(lq2_ref[...] * lk2_ref[...], axis=1, keepdims=True)) + lam_init)
    outs = []
    for h in range(A_HEADS):
        acc = acca_ref[h]
        d = _normalized(acc, 0) - lam * _normalized(acc, TQ)
        ms = jnp.mean(d * d, axis=0, keepdims=True)
        outs.append(((d * lax.rsqrt(ms + RMS_EPS)) * sub_ref[...]) * (1.0 - lam_init))
    emit(0, outs)

    outs = []
    for c in range(B_HEADS):
        outs.append(_normalized(accb_ref[c // 2], (c % 2) * TQ))
    emit(A_W, outs)

    sig = 1.0 / (1.0 + jnp.exp(-glt_ref[...]))
    outs = []
    for c in range(C_HEADS):
        g, i = c // cpg, c % cpg
        gate = lambda br: sig[br * C_HEADS + c:br * C_HEADS + c + 1, :]
        lane0 = (c % 2) * TQ
        outs.append(gate(0) * o_cmp[g][:, i * TQ:(i + 1) * TQ]
                    + gate(1) * _normalized(accc_ref[0, c // 2], lane0)
                    + gate(2) * _normalized(accc_ref[1, c // 2], lane0))
    emit(A_W + B_W, outs)


def _overlap_t(ncmp_pad, n_slc):
    cs = np.arange(ncmp_pad)[None, :] * CMP_STRIDE
    ss = np.arange(n_slc)[:, None] * SLC_LEN
    ov = np.minimum(cs + CMP_LEN, ss + SLC_LEN) - np.maximum(cs, ss)
    ov = np.maximum(ov, 0) / CMP_LEN
    ov[:, ncmp_pad - 1] = 0.0
    return ov.astype(np.float32)


def _mixers(sinks, hk3, hg3, ht, gt, kcmp, vcmpt, lq1, lk1, lq2, lk2, sub, lam_init):
    b, s, _ = hk3.shape
    assert B_WINDOW + TQ <= TK and B_HEADS <= SUBLANES
    n_slc = s // SLC_LEN
    ncmp = kcmp.shape[1]
    nq = s // TQ
    ovt = jnp.asarray(_overlap_t(ncmp, n_slc), dtype=BF16)
    kx = _key_ext(s)
    qxa = _query_ext([_SLOPE_A[c // 2] for c in range(2 * A_HEADS)])
    qxb = _query_ext(_SLOPE_B)
    qxc = _query_ext(_SLOPE_C)
    const = lambda a: pl.BlockSpec(a.shape, lambda bi, qi: (0,) * a.ndim)
    qt = lambda off, rows: pl.BlockSpec((rows, TQ), lambda bi, qi: (off // rows, bi * nq + qi))
    vt = lambda off, rows: pl.BlockSpec((rows, s), lambda bi, qi: (off // rows, bi))
    krow = lambda off, cols: pl.BlockSpec((None, s, cols), lambda bi, qi: (bi, 0, off // cols))
    gate = lambda off, cols: pl.BlockSpec((None, TQ, cols), lambda bi, qi: (bi, qi, off // cols))
    per_batch = lambda a: pl.BlockSpec((None,) + a.shape[1:], lambda bi, qi: (bi, 0, 0))
    return pl.pallas_call(
        functools.partial(_mixers_kernel, lam_init=lam_init),
        grid=(b, nq),
        in_specs=[pl.BlockSpec(memory_space=pltpu.SMEM),
                  qt(T_QA, A_W), qt(T_QB, B_W), qt(T_QC, C_W),
                  krow(K_KA, A_W), krow(K_KB, KV_B), krow(K_KS, KV_C), krow(K_KW, KV_C),
                  vt(T_VA, A_W), vt(T_VB, KV_B), vt(T_VS, KV_C), vt(T_VW, KV_C),
                  per_batch(kcmp), per_batch(vcmpt),
                  gate(G_GA, A_W), gate(G_GB, B_W), gate(G_GC, C_W),
                  pl.BlockSpec((NGT, TQ), lambda bi, qi: (0, bi * nq + qi)),
                  const(lq1), const(lk1), const(lq2), const(lk2), const(sub), const(ovt),
                  const(kx), const(qxa), const(qxb), const(qxc)],
        out_specs=pl.BlockSpec((None, TQ, D_MIX), lambda bi, qi: (bi, qi, 0)),
        out_shape=jax.ShapeDtypeStruct((b, s, D_MIX), BF16),
        scratch_shapes=[pltpu.VMEM((A_HEADS, 2 * LANES, 2 * TQ), BF16),
                        pltpu.VMEM((B_HEADS // 2, 2 * LANES, 2 * TQ), BF16),
                        pltpu.VMEM((C_HEADS // 2, 2 * LANES, 2 * TQ), BF16),
                        pltpu.VMEM((A_HEADS, V_ROWS, s), BF16),
                        pltpu.VMEM((B_KV, V_ROWS, s), BF16),
                        pltpu.VMEM((2, C_KV, V_ROWS, s), BF16),
                        pltpu.VMEM((TK, 2 * A_HEADS * TQ), BF16),
                        pltpu.VMEM((TK, B_HEADS * TQ), BF16),
                        pltpu.VMEM((TK, C_HEADS * TQ), BF16),
                        pltpu.VMEM((A_HEADS, V_ROWS, 2 * TQ), F32),
                        pltpu.VMEM((B_HEADS // 2, V_ROWS, 2 * TQ), F32),
                        pltpu.VMEM((2, C_HEADS // 2, V_ROWS, 2 * TQ), F32),
                        pltpu.VMEM((2 * A_HEADS, TQ), F32),
                        pltpu.VMEM((SUBLANES, TQ), F32),
                        pltpu.VMEM((2, C_HEADS, TQ), F32),
                        pltpu.VMEM((C_KV, s // TK, BF16_ROWS, TQ), BF16),
                        pltpu.VMEM((AHEAD, TK, 2 * TQ), F32),
                        pltpu.VMEM((TQ, D_MIX), F32)],
        compiler_params=_params("arbitrary", "arbitrary"),
        name="mixers",
    )(sinks, ht, ht, ht, hk3, hk3, hk3, hk3, ht, ht, ht, ht, kcmp, vcmpt, hg3, hg3, hg3, gt,
      lq1, lk1, lq2, lk2, sub, ovt, kx, qxa, qxb, qxc)


def kernel(x, norm_w, w_in, w_out, diff_lq1, diff_lk1, diff_lq2, diff_lk2, diff_subln, sinks,
           cmp_pe_k, cmp_pe_v, cmp_wk1, cmp_wk2, cmp_wv1, cmp_wv2, final_norm):
    b, s, d = x.shape
    depth = w_in.shape[0]
    assert s % TK == 0 and s // TK >= 3 and (b * s) % TM == 0 and d % LANES == 0

    w_rm, w_t = _prep_w_in(w_in)
    w_out_b = w_out.astype(BF16)
    pe2 = lambda pe: jnp.concatenate([pe, pe], axis=-1)
    w1 = lambda w: jnp.concatenate([w.reshape(depth, CMP_LEN, HEAD_DIM, CMP_HID)] * 2, axis=2).astype(BF16)
    zero2 = jnp.zeros((depth, CMP_HID, HEAD_DIM), F32)
    w2 = lambda w: jnp.stack([jnp.concatenate([w, zero2], axis=-1),
                              jnp.concatenate([zero2, w], axis=-1)], axis=1)
    pek, pev = pe2(cmp_pe_k), pe2(cmp_pe_v)
    wk1, wv1 = w1(cmp_wk1), w1(cmp_wv1)
    wk2 = w2(cmp_wk2).astype(BF16)
    wv2t = jnp.swapaxes(w2(cmp_wv2), 2, 3).astype(BF16)
    sub = jnp.broadcast_to(diff_subln[:, :, None], (depth, HEAD_DIM, TQ))
    fw = final_norm[None, :]
    x2 = x.reshape(b * s, d)
    for l in range(depth):
        hk, hg, ht, gt = _in_proj(x2, norm_w[l][None, :], w_rm[l], w_t[l])
        hk3 = hk.reshape(b, s, NK)
        hg3 = hg.reshape(b, s, NG)
        kcmp, vcmpt = _compress(hg3, pek[l], pev[l], wk1[l], wk2[l], wv1[l], wv2t[l])
        lam_init = 0.8 - 0.6 * math.exp(-0.3 * l)
        y = _mixers(sinks[l], hk3, hg3, ht, gt, kcmp, vcmpt, diff_lq1[l][None, :], diff_lk1[l][None, :],
                    diff_lq2[l][None, :], diff_lk2[l][None, :], sub[l], lam_init)
        x2 = _out_proj(y.reshape(b * s, D_MIX), w_out_b[l], x2, fw, l == depth - 1)
    return x2.reshape(b, s, d)
```

```python
import functools
import math

import numpy as np
import jax
import jax.numpy as jnp
from jax import lax
from jax.experimental import pallas as pl
from jax.experimental.pallas import tpu as pltpu

F32 = jnp.float32
BF16 = jnp.bfloat16

HEAD_DIM = 64
RMS_EPS = 1e-6
NEG = -1e30
FORCE = 1e30

A_HEADS = 4
A_QK = HEAD_DIM // 2
B_HEADS = 4
B_KV = 2
B_WINDOW = 128
C_HEADS = 8
C_KV = 2
CMP_LEN = 32
CMP_STRIDE = 16
CMP_HID = 128
SLC_LEN = 64
SLC_TOPN = 8
C_WINDOW = 512

A_W = A_HEADS * HEAD_DIM
B_W = B_HEADS * HEAD_DIM
C_W = C_HEADS * HEAD_DIM
D_MIX = A_W + B_W + C_W
KV_B = B_KV * HEAD_DIM
KV_C = C_KV * HEAD_DIM
N_GATES = 3 * C_HEADS

LANES = 128
SUBLANES = 8
BF16_ROWS = 16
TQ = 128
TK = 256
TM = 512
AHEAD = 4
V_ROWS = HEAD_DIM + BF16_ROWS
VMEM_LIMIT = 48 * 1024 * 1024

LOG2E = math.log2(math.e)

_O = dict(qa=0, ka=256, va=512, ga=768, qb=1024, kb=1280, vb=1408, gb=1536, qc=1792,
          kc=2304, vc=2432, ks=2560, vs=2688, kw=2816, vw=2944, gl=3072, gc=3096)

K_KA, K_KB, K_KS, K_KW = 0, 256, 384, 512
NK = 640
G_GA, G_GB, G_GC, G_KC, G_VC = 0, 256, 512, 1024, 1152
NG = 1280
T_QA, T_QB, T_QC, T_VA, T_VB, T_VS, T_VW = 0, 256, 512, 1024, 1280, 1408, 1536
NT = 1664
NGT = 32


def _slopes(n):
    return [float(2.0 ** (-8.0 * h / n)) * LOG2E for h in range(1, n + 1)]


_SLOPE_A = _slopes(A_HEADS)
_SLOPE_B = _slopes(B_HEADS)
_SLOPE_C = _slopes(C_HEADS)


def _params(*sem):
    return pltpu.CompilerParams(dimension_semantics=sem, vmem_limit_bytes=VMEM_LIMIT)


def _dot_nt(a, b):
    return lax.dot_general(a, b, (((1,), (1,)), ((), ())), preferred_element_type=F32)


def _dot(a, b):
    return jnp.dot(a, b, preferred_element_type=F32)


def _silu(g):
    return g * (1.0 / (1.0 + jnp.exp(-g)))


def _prep_w_in(w_in):
    depth, d, _ = w_in.shape
    col = lambda name, width: w_in[:, :, _O[name]:_O[name] + width]
    w_rm = jnp.concatenate([col('ka', A_W), col('kb', KV_B), col('ks', KV_C), col('kw', KV_C),
                            col('ga', A_W), col('gb', B_W), col('gc', C_W), col('kc', KV_C), col('vc', KV_C)],
                           axis=2).astype(BF16)
    gl = jnp.swapaxes(col('gl', N_GATES).reshape(depth, d, C_HEADS, 3), 2, 3).reshape(depth, d, N_GATES)
    gl = jnp.concatenate([gl, jnp.zeros((depth, d, NGT - N_GATES), w_in.dtype)], axis=2)
    cols_t = jnp.concatenate([col('qa', A_W) * (A_QK ** -0.5 * LOG2E),
                              col('qb', B_W) * (HEAD_DIM ** -0.5 * LOG2E),
                              col('qc', C_W) * (HEAD_DIM ** -0.5 * LOG2E),
                              col('va', A_W), col('vb', KV_B), col('vs', KV_C), col('vw', KV_C), gl], axis=2)
    w_t = jnp.swapaxes(cols_t, 1, 2).astype(BF16)
    return w_rm, w_t


EXT_ROWS = 2 * BF16_ROWS
POS_SPLIT = 256


def _bf16_pieces(x, n=3):
    out, rest = [], np.float32(x)
    for _ in range(n):
        bits = np.array([rest], np.float32).view(np.uint32)
        bits = (bits + 0x7FFF + ((bits >> 16) & 1)) & np.uint32(0xFFFF0000)
        piece = bits.view(np.float32)[0]
        out.append(piece)
        rest = np.float32(rest - piece)
    return out


def _key_ext(s):
    pos = np.arange(s)
    ext = np.zeros((s, 2, LANES), np.float32)
    for j in range(3):
        ext[:, :, j] = (pos // POS_SPLIT)[:, None]
        ext[:, :, 3 + j] = (pos % POS_SPLIT)[:, None]
    blk = (pos % TK) // SLC_LEN
    ext[pos, 1, BF16_ROWS + blk] = 1.0
    return jnp.asarray(ext.reshape(s, 2 * LANES), dtype=BF16)


def _query_ext(slopes):
    ext = np.zeros((len(slopes) // 2, EXT_ROWS, 2, TQ), np.float32)
    for c, slope in enumerate(slopes):
        for j, piece in enumerate(_bf16_pieces(slope)):
            ext[c // 2, j, c % 2, :] = piece * POS_SPLIT
            ext[c // 2, 3 + j, c % 2, :] = piece
    return jnp.asarray(ext.reshape(len(slopes) // 2, EXT_ROWS, 2 * TQ), dtype=BF16)


def _in_proj_kernel(*refs, fused):
    if fused:
        y_ref, wo_ref, x_ref, nw_ref, w_ref, wt_ref, xo_ref, hk_ref, hg_ref, ht_ref, gt_ref = refs
        x = x_ref[...] + _dot(y_ref[...], wo_ref[...])
        xo_ref[...] = x
    else:
        x_ref, nw_ref, w_ref, wt_ref, hk_ref, hg_ref, ht_ref, gt_ref = refs
        x = x_ref[...]
    ms = jnp.mean(x * x, axis=-1, keepdims=True)
    xn = ((x * lax.rsqrt(ms + RMS_EPS)) * nw_ref[...]).astype(BF16)
    step = 256
    for c0 in range(0, NK, step):
        c1 = min(c0 + step, NK)
        hk_ref[:, c0:c1] = _dot(xn, w_ref[:, c0:c1]).astype(BF16)
    for c0 in range(0, NG, step):
        c1 = min(c0 + step, NG)
        hg_ref[:, c0:c1] = _dot(xn, w_ref[:, NK + c0:NK + c1])
    for r0 in range(0, NT, step):
        r1 = min(r0 + step, NT)
        ht_ref[r0:r1, :] = _dot_nt(wt_ref[r0:r1, :], xn).astype(BF16)
    gt_ref[...] = _dot_nt(wt_ref[NT:NT + NGT, :], xn)


def _layer(a, l):
    return pl.BlockSpec((None,) + a.shape[1:], lambda *_: (l,) + (0,) * (a.ndim - 1))


def _in_proj(l, x2, nw, w_rm, w_t, y=None, w_out=None):
    n, d = x2.shape
    fused = y is not None
    row = lambda cols: pl.BlockSpec((TM, cols), lambda i: (i, 0))
    in_specs = [row(d), _layer(nw, l), _layer(w_rm, l), _layer(w_t, l)]
    out_specs = [row(NK), row(NG), pl.BlockSpec((NT, TM), lambda i: (0, i)), pl.BlockSpec((NGT, TM), lambda i: (0, i))]
    out_shape = [jax.ShapeDtypeStruct((n, NK), BF16), jax.ShapeDtypeStruct((n, NG), F32),
                 jax.ShapeDtypeStruct((NT, n), BF16), jax.ShapeDtypeStruct((NGT, n), F32)]
    args = (x2, nw, w_rm, w_t)
    if fused:
        in_specs = [row(D_MIX), _layer(w_out, l - 1)] + in_specs
        out_specs = [row(d)] + out_specs
        out_shape = [jax.ShapeDtypeStruct((n, d), F32)] + out_shape
        args = (y, w_out) + args
    return pl.pallas_call(
        functools.partial(_in_proj_kernel, fused=fused),
        grid=(n // TM,),
        in_specs=in_specs, out_specs=out_specs, out_shape=out_shape,
        compiler_params=_params("arbitrary"),
        name="out_in_proj" if fused else "in_proj",
    )(*args)


def _out_proj_kernel(y_ref, w_ref, x_ref, fw_ref, o_ref):
    acc = x_ref[...] + _dot(y_ref[...], w_ref[...])
    ms = jnp.mean(acc * acc, axis=-1, keepdims=True)
    o_ref[...] = (acc * lax.rsqrt(ms + RMS_EPS)) * fw_ref[...]


def _out_proj(l, y, w_out, x2, fw):
    n, d = x2.shape
    return pl.pallas_call(
        _out_proj_kernel,
        grid=(n // TM,),
        in_specs=[pl.BlockSpec((TM, D_MIX), lambda i: (i, 0)),
                  _layer(w_out, l),
                  pl.BlockSpec((TM, d), lambda i: (i, 0)),
                  pl.BlockSpec((1, d), lambda i: (0, 0))],
        out_specs=pl.BlockSpec((TM, d), lambda i: (i, 0)),
        out_shape=jax.ShapeDtypeStruct((n, d), F32),
        compiler_params=_params("arbitrary"),
        name="out_proj",
    )(y, w_out, x2, fw)


def _compress_kernel(kc_ref, vc_ref, pek_ref, pev_ref, wk1_ref, wk2_ref, wv1_ref, wv2_ref,
                     kcmp_ref, vcmpt_ref):
    nch = kc_ref.shape[0] // CMP_STRIDE
    left = lax.broadcasted_iota(jnp.int32, (nch, LANES), 1) < HEAD_DIM

    def one(x_ref, pe_ref, w1_ref, w2_ref, out_ref, transposed):
        u = [jnp.zeros((nch, CMP_HID), F32) for _ in range(C_KV)]
        v = [jnp.zeros((nch, CMP_HID), F32) for _ in range(C_KV)]
        for l in range(CMP_STRIDE):
            xl = x_ref[pl.ds(l, nch, stride=CMP_STRIDE), :]
            pe_u = pe_ref[l:l + 1, :]
            pe_v = pe_ref[CMP_STRIDE + l:CMP_STRIDE + l + 1, :]
            w_u = w1_ref[l]
            w_v = w1_ref[CMP_STRIDE + l]
            for g in range(C_KV):
                keep = left if g == 0 else jnp.logical_not(left)
                u[g] = u[g] + _dot(jnp.where(keep, xl + pe_u, 0.0).astype(BF16), w_u)
                v[g] = v[g] + _dot(jnp.where(keep, xl + pe_v, 0.0).astype(BF16), w_v)
        acc = jnp.zeros((LANES, nch) if transposed else (nch, LANES), F32)
        for g in range(C_KV):
            hid = u[g] + pltpu.roll(v[g], nch - 1, 0)
            act = _silu(hid).astype(BF16)
            acc = acc + (_dot_nt(w2_ref[g], act) if transposed else _dot(act, w2_ref[g]))
        out_ref[...] = acc.astype(BF16)

    one(kc_ref, pek_ref, wk1_ref, wk2_ref, kcmp_ref, False)
    one(vc_ref, pev_ref, wv1_ref, wv2_ref, vcmpt_ref, True)


def _compress(l, hg3, pek, pev, wk1, wk2, wv1, wv2t):
    b, s, _ = hg3.shape
    nch = s // CMP_STRIDE
    full = lambda a: _layer(a, l)
    return pl.pallas_call(
        _compress_kernel,
        grid=(b,),
        in_specs=[pl.BlockSpec((None, s, LANES), lambda i: (i, 0, G_KC // LANES)),
                  pl.BlockSpec((None, s, LANES), lambda i: (i, 0, G_VC // LANES)),
                  full(pek), full(pev), full(wk1), full(wk2), full(wv1), full(wv2t)],
        out_specs=[pl.BlockSpec((None, nch, LANES), lambda i: (i, 0, 0)),
                   pl.BlockSpec((None, LANES, nch), lambda i: (i, 0, 0))],
        out_shape=[jax.ShapeDtypeStruct((b, nch, LANES), BF16), jax.ShapeDtypeStruct((b, LANES, nch), BF16)],
        compiler_params=_params("arbitrary"),
        name="compress",
    )(hg3, hg3, pek, pev, wk1, wk2, wv1, wv2t)


def _fill_value_rows(vaug_ref, vt_ref, n_heads):
    s = vt_ref.shape[1]
    for h in range(n_heads):
        vaug_ref[h, 0:HEAD_DIM, :] = vt_ref[h * HEAD_DIM:(h + 1) * HEAD_DIM, :]
        vaug_ref[h, HEAD_DIM:V_ROWS, :] = jnp.ones((BF16_ROWS, s), BF16)


def _stack_gqa_q(qt_ref, qst_ref, heads_per_group):
    zero = jnp.zeros((HEAD_DIM, TQ), BF16)
    for p in range(qst_ref.shape[0]):
        cols = []
        for h in (2 * p, 2 * p + 1):
            q = qt_ref[h * HEAD_DIM:(h + 1) * HEAD_DIM, :]
            cols.append(jnp.concatenate([q, zero] if h // heads_per_group == 0 else [zero, q], axis=0))
        qst_ref[p, 0:LANES, :] = jnp.concatenate(cols, axis=1)


def _set_ext_rows(qst_ref, qx_ref):
    n = qst_ref.shape[0]
    qst_ref[:, LANES:LANES + EXT_ROWS, :] = qx_ref[...]
    qst_ref[:, LANES + EXT_ROWS:, :] = jnp.zeros((n, LANES - EXT_ROWS, 2 * TQ), BF16)


def _ext_keys(k_chunk, kx_ref, s0, selected):
    c0 = LANES if selected else 0
    return jnp.concatenate([k_chunk, kx_ref[pl.ds(s0, k_chunk.shape[0]), c0:c0 + LANES]], axis=1)


def _flash_pair(st, c0, madd, m_ref, e_ref):
    rows = st.shape[0]
    alphas = []
    for j in range(2):
        c = c0 + j
        u = st[:, j * TQ:(j + 1) * TQ]
        if madd is not None:
            u = u + madd
        m_old = m_ref[c:c + 1, :]
        m_new = jnp.maximum(m_old, jnp.max(u, axis=0, keepdims=True))
        e_ref[0:rows, c * TQ:(c + 1) * TQ] = jnp.exp2(u - m_new).astype(BF16)
        alphas.append(jnp.exp2(m_old - m_new))
        m_ref[c:c + 1, :] = m_new
    return alphas


def _accumulate(acc_ref, idx, vaug_chunk, e_ref, c0, alphas):
    a = jnp.concatenate(alphas, axis=1)
    rows = vaug_chunk.shape[1]
    acc_ref[idx] = acc_ref[idx] * a + _dot(vaug_chunk, e_ref[0:rows, c0 * TQ:(c0 + 2) * TQ])


def _pipelined(items, st_ref=None, lookahead=()):
    if st_ref is None:
        queue = [scores() for scores, _ in items[:AHEAD]]
    else:
        assert len(items) >= AHEAD and len(lookahead) in (0, AHEAD)
        queue = [functools.partial(lambda j: st_ref[j], j) for j in range(AHEAD)]
    todo = [scores for scores, _ in items[AHEAD:]] + [scores for scores, _ in lookahead]
    for _, consume in items:
        if todo:
            queue.append(todo.pop(0)())
        st = queue.pop(0)
        consume(st() if callable(st) else st)
    for j, st in enumerate(queue):
        st_ref[j] = st


def _normalized(acc, lane0):
    return acc[0:HEAD_DIM, lane0:lane0 + TQ] / acc[HEAD_DIM:HEAD_DIM + 1, lane0:lane0 + TQ]


def _split3(x):
    hi = x.astype(BF16)
    r1 = x - hi.astype(F32)
    mid = r1.astype(BF16)
    lo = (r1 - mid.astype(F32)).astype(BF16)
    return hi, mid, lo


def _mixers_kernel(sink_ref,
                   qa_ref, qb_ref, qc_ref, ka_ref, kb_ref, ks_ref, kw_ref, va_ref, vb_ref, vs_ref, vw_ref,
                   kcmp_ref, vcmpt_ref, ga_ref, gb_ref, gc_ref, glt_ref,
                   lq1_ref, lk1_ref, lq2_ref, lk2_ref, sub_ref, ovt_ref, kx_ref, qxa_ref, qxb_ref, qxc_ref,
                   y_ref,
                   qsta_ref, qstb_ref, qstc_ref, vauga_ref, vaugb_ref, vaugc_ref,
                   ea_ref, eb_ref, ec_ref, acca_ref, accb_ref, accc_ref, ma_ref, mb_ref, mc_ref,
                   msel_ref, st_ref, gsil_ref, *, layer, lam_init):
    qi = pl.program_id(1)
    s_len = ka_ref.shape[0]
    n_slc = s_len // SLC_LEN
    ncmp = kcmp_ref.shape[0]
    blocks_per_chunk = TK // SLC_LEN
    cpg = C_HEADS // C_KV

    @pl.when(qi == 0)
    def _():
        _fill_value_rows(vauga_ref, va_ref, A_HEADS)
        _fill_value_rows(vaugb_ref, vb_ref, B_KV)
        _fill_value_rows(vaugc_ref.at[0], vs_ref, C_KV)
        _fill_value_rows(vaugc_ref.at[1], vw_ref, C_KV)

    sub32 = lax.broadcasted_iota(jnp.int32, (LANES, TQ), 0) // A_QK
    for h in range(A_HEADS):
        blk = h * HEAD_DIM // LANES
        q = qa_ref[blk * LANES:(blk + 1) * LANES, :]
        zero = jnp.zeros_like(q)
        first = 2 * (h % (LANES // HEAD_DIM))
        qsta_ref[h, 0:LANES, :] = jnp.concatenate(
            [jnp.where(sub32 == first, q, zero), jnp.where(sub32 == first + 1, q, zero)], axis=1)
    _stack_gqa_q(qb_ref, qstb_ref, B_HEADS // B_KV)
    _stack_gqa_q(qc_ref, qstc_ref, cpg)
    _set_ext_rows(qsta_ref, qxa_ref)
    _set_ext_rows(qstb_ref, qxb_ref)
    _set_ext_rows(qstc_ref, qxc_ref)
    acca_ref[...] = jnp.zeros(acca_ref.shape, F32)
    accc_ref[...] = jnp.zeros(accc_ref.shape, F32)
    t0 = qi * TQ
    den_rows = lax.broadcasted_iota(jnp.int32, (V_ROWS, 2 * TQ), 0) >= HEAD_DIM
    for g in range(B_KV):
        accb_ref[g] = jnp.where(den_rows, 1.0, 0.0)
    t_abs = (t0 + lax.broadcasted_iota(jnp.int32, (1, TQ), 1)).astype(F32)
    for c in range(B_HEADS):
        mb_ref[c:c + 1, :] = sink_ref[layer, c] * LOG2E + _SLOPE_B[c] * t_abs
    ma_ref[...] = jnp.full(ma_ref.shape, NEG, F32)
    mc_ref[...] = jnp.full(mc_ref.shape, NEG, F32)
    row_i = lax.broadcasted_iota(jnp.int32, (TK, TQ), 0)
    row_f = row_i.astype(F32)
    kmq = (row_i - lax.broadcasted_iota(jnp.int32, (TK, TQ), 1)).astype(F32)

    def start(ch):
        return pl.multiple_of(ch * TK, TK)

    masks = {}

    def cached(tag, build):
        if tag not in masks:
            masks[tag] = build()
        return masks[tag]

    def causal_mask(tag, s0):
        return cached(tag, lambda: jnp.where(kmq + (s0 - t0).astype(F32) <= 0.0, 0.0, NEG))

    def make_item(keys, weights, rows, madd, m_ref, e_ref, c0, acc_ref, idx, values):
        def scores():
            return _dot(keys(), weights())

        def consume(st):
            alphas = _flash_pair(st[0:rows], c0, madd(), m_ref, e_ref)
            _accumulate(acc_ref, idx, values(), e_ref, c0, alphas)

        return scores, consume

    def item_a(tag, ch, diagonal, h):
        blk = h * HEAD_DIM // LANES
        return make_item(
            lambda: _ext_keys(ka_ref[pl.ds(start(ch), TK), blk * LANES:(blk + 1) * LANES], kx_ref, start(ch), False),
            lambda: qsta_ref[h], TK,
            lambda: causal_mask(tag, start(ch)) if diagonal else None,
            ma_ref, ea_ref, 2 * h, acca_ref, h, lambda: vauga_ref[h, :, pl.ds(start(ch), TK)])

    def item_s(tag, ch, diagonal, p):
        g = 2 * p // cpg

        def weights():
            rows = msel_ref[g, ch]
            qstc_ref[p, LANES + BF16_ROWS:LANES + EXT_ROWS, :] = jnp.concatenate([rows, rows], axis=1)
            return qstc_ref[p]

        return make_item(
            lambda: _ext_keys(ks_ref[pl.ds(start(ch), TK), :], kx_ref, start(ch), True), weights, TK,
            lambda: causal_mask(tag, start(ch)) if diagonal else None,
            mc_ref.at[0], ec_ref, 2 * p, accc_ref.at[0], p, lambda: vaugc_ref[0, g, :, pl.ds(start(ch), TK)])

    def item_w(r, p):
        g = 2 * p // cpg
        n_full = C_WINDOW // TK
        rows = TK if r < n_full else TQ
        nominal = t0 - C_WINDOW + r * TK
        s0 = pl.multiple_of(jnp.maximum(nominal, 0), TQ)

        def madd():
            if r == n_full:
                return cached('own', lambda: jnp.where(kmq[0:rows] <= 0.0, 0.0, NEG))

            def build():
                ok = row_f < (nominal + TK - s0).astype(F32)
                if r == 0:
                    ok = jnp.logical_and(ok, kmq + (s0 - t0).astype(F32) > -float(C_WINDOW))
                return jnp.where(ok, 0.0, NEG)
            return cached(('win', r), build)

        return make_item(
            lambda: _ext_keys(kw_ref[pl.ds(s0, rows), :], kx_ref, s0, False), lambda: qstc_ref[p], rows, madd,
            mc_ref.at[1], ec_ref, 2 * p, accc_ref.at[1], p, lambda: vaugc_ref[1, g, :, pl.ds(s0, rows)])

    def item_b(g):
        s0 = pl.multiple_of(jnp.maximum(t0 - B_WINDOW, 0), LANES)

        def madd():
            nd = kmq + (s0 - t0).astype(F32)
            return jnp.where(jnp.logical_and(nd <= 0.0, nd > -float(B_WINDOW)), 0.0, NEG)

        return make_item(
            lambda: _ext_keys(kb_ref[pl.ds(s0, TK), :], kx_ref, s0, False), lambda: qstb_ref[g], TK, madd,
            mb_ref, eb_ref, 2 * g, accb_ref, g, lambda: vaugb_ref[g, :, pl.ds(s0, TK)])

    def causal_items(tag, ch, diagonal):
        return ([item_a(tag, ch, diagonal, h) for h in range(A_HEADS)]
                + [item_s(tag, ch, diagonal, p) for p in range(C_HEADS // 2)])

    last = (t0 + TQ - 1) // TK
    first = causal_items('first', 0, False)[:AHEAD]
    for j, (scores, _) in enumerate(first[:A_HEADS]):
        st_ref[j] = scores()
    for col0, g_ref in ((0, ga_ref), (A_W, gb_ref), (A_W + B_W, gc_ref)):
        gsil_ref[:, col0:col0 + g_ref.shape[1]] = _silu(g_ref[...])

    p_cmp = []
    tpos = t0 + lax.broadcasted_iota(jnp.int32, (ncmp, TQ), 1)
    cend = lax.broadcasted_iota(jnp.int32, (ncmp, TQ), 0) * CMP_STRIDE + (CMP_LEN - 1)
    dist_c = (tpos - cend).astype(F32)
    valid_c = dist_c >= 0.0
    for p in range(C_HEADS // 2):
        st = _dot(kcmp_ref[...], qstc_ref[p, 0:LANES, :])
        for j in range(2):
            c = 2 * p + j
            sc = jnp.where(valid_c, st[:, j * TQ:(j + 1) * TQ] - _SLOPE_C[c] * dist_c, NEG)
            mx = jnp.max(sc, axis=0, keepdims=True)
            e = jnp.where(valid_c, jnp.exp2(sc - mx), 0.0)
            den = jnp.sum(e, axis=0, keepdims=True)
            p_cmp.append(e / jnp.where(den > 0.0, den, 1.0))
    o_cmp = [_dot(vcmpt_ref[g * HEAD_DIM:(g + 1) * HEAD_DIM, :],
                  jnp.concatenate(p_cmp[g * cpg:(g + 1) * cpg], axis=1).astype(BF16))
             for g in range(C_KV)]

    jidx = lax.broadcasted_iota(jnp.int32, (n_slc, TQ), 0)
    tl = t0 + lax.broadcasted_iota(jnp.int32, (n_slc, TQ), 1)
    cur = tl // SLC_LEN
    causal_b = jidx * SLC_LEN <= tl
    forced = jnp.logical_or(jidx == 0, jnp.logical_or(jidx == cur, jidx == cur - 1))
    for g in range(C_KV):
        psum = p_cmp[g * cpg]
        for i in range(1, cpg):
            psum = psum + p_cmp[g * cpg + i]
        imp = jnp.zeros((n_slc, TQ), F32)
        for part in _split3(psum):
            imp = imp + _dot(ovt_ref[...], part)
        score = jnp.where(causal_b, jnp.where(forced, FORCE, imp), NEG)
        rank = jnp.zeros((n_slc, TQ), F32)
        for jp in range(n_slc):
            row = score[jp:jp + 1, :]
            tie = jnp.where(jidx > jp, 1.0, 0.0)
            rank = rank + jnp.where(row > score, 1.0, jnp.where(row == score, tie, 0.0))
        msel = jnp.where(rank < float(SLC_TOPN), 0.0, NEG)
        pad = jnp.zeros((BF16_ROWS - blocks_per_chunk, TQ), F32)
        for ch in range(n_slc // blocks_per_chunk):
            rows = msel[ch * blocks_per_chunk:(ch + 1) * blocks_per_chunk, :]
            msel_ref[g, ch] = jnp.concatenate([rows, pad], axis=0).astype(BF16)
    for j, (scores, _) in enumerate(first[A_HEADS:]):
        st_ref[A_HEADS + j] = scores()

    def two_chunks(i, carry):
        masks.clear()
        _pipelined(causal_items(0, 2 * i, False) + causal_items(1, 2 * i + 1, False), st_ref,
                   causal_items(2, 2 * i + 2, False)[:AHEAD])
        masks.clear()
        return carry

    lax.fori_loop(0, last // 2, two_chunks, 0)

    def tail_items():
        return ([item_w(r, p) for r in range(C_WINDOW // TK + 1) for p in range(C_HEADS // 2)]
                + [item_b(g) for g in range(B_KV)])

    @pl.when(last % 2 == 1)
    def _():
        masks.clear()
        _pipelined(causal_items(0, last - 1, False) + causal_items(1, last, True) + tail_items(), st_ref)
        masks.clear()

    @pl.when(last % 2 == 0)
    def _():
        masks.clear()
        _pipelined(causal_items(1, last, True) + tail_items(), st_ref)
        masks.clear()

    def emit(col0, outs):
        for i in range(len(outs) // 2):
            o = jnp.transpose(jnp.concatenate(outs[2 * i:2 * i + 2], axis=0))
            cols = slice(col0 + i * LANES, col0 + (i + 1) * LANES)
            y_ref[:, cols] = (o * gsil_ref[:, cols]).astype(BF16)

    lam = (jnp.exp(jnp.sum(lq1_ref[...] * lk1_ref[...], axis=1, keepdims=True))
           - jnp.exp(jnp.sum(lq2_ref[...] * lk2_ref[...], axis=1, keepdims=True)) + lam_init)
    outs = []
    for h in range(A_HEADS):
        acc = acca_ref[h]
        d = _normalized(acc, 0) - lam * _normalized(acc, TQ)
        ms = jnp.mean(d * d, axis=0, keepdims=True)
        outs.append(((d * lax.rsqrt(ms + RMS_EPS)) * sub_ref[...]) * (1.0 - lam_init))
    emit(0, outs)

    outs = []
    for c in range(B_HEADS):
        outs.append(_normalized(accb_ref[c // 2], (c % 2) * TQ))
    emit(A_W, outs)

    sig = 1.0 / (1.0 + jnp.exp(-glt_ref[...]))
    outs = []
    for c in range(C_HEADS):
        g, i = c // cpg, c % cpg
        gate = lambda br: sig[br * C_HEADS + c:br * C_HEADS + c + 1, :]
        lane0 = (c % 2) * TQ
        outs.append(gate(0) * o_cmp[g][:, i * TQ:(i + 1) * TQ]
                    + gate(1) * _normalized(accc_ref[0, c // 2], lane0)
                    + gate(2) * _normalized(accc_ref[1, c // 2], lane0))
    emit(A_W + B_W, outs)


def _overlap_t(ncmp_pad, n_slc):
    cs = np.arange(ncmp_pad)[None, :] * CMP_STRIDE
    ss = np.arange(n_slc)[:, None] * SLC_LEN
    ov = np.minimum(cs + CMP_LEN, ss + SLC_LEN) - np.maximum(cs, ss)
    ov = np.maximum(ov, 0) / CMP_LEN
    ov[:, ncmp_pad - 1] = 0.0
    return ov.astype(np.float32)


def _mixers(l, sinks, hk3, hg3, ht, gt, kcmp, vcmpt, lq1, lk1, lq2, lk2, sub, lam_init):
    b, s, _ = hk3.shape
    assert B_WINDOW + TQ <= TK and B_HEADS <= SUBLANES and C_WINDOW % TK == 0 and TQ % BF16_ROWS == 0
    n_slc = s // SLC_LEN
    ncmp = kcmp.shape[1]
    nq = s // TQ
    ovt = jnp.asarray(_overlap_t(ncmp, n_slc), dtype=BF16)
    kx = _key_ext(s)
    qxa = _query_ext([_SLOPE_A[c // 2] for c in range(2 * A_HEADS)])
    qxb = _query_ext(_SLOPE_B)
    qxc = _query_ext(_SLOPE_C)
    const = lambda a: pl.BlockSpec(a.shape, lambda bi, qi: (0,) * a.ndim)
    qt = lambda off, rows: pl.BlockSpec((rows, TQ), lambda bi, qi: (off // rows, bi * nq + qi))
    vt = lambda off, rows: pl.BlockSpec((rows, s), lambda bi, qi: (off // rows, bi))
    krow = lambda off, cols: pl.BlockSpec((None, s, cols), lambda bi, qi: (bi, 0, off // cols))
    gate = lambda off, cols: pl.BlockSpec((None, TQ, cols), lambda bi, qi: (bi, qi, off // cols))
    per_batch = lambda a: pl.BlockSpec((None,) + a.shape[1:], lambda bi, qi: (bi, 0, 0))
    return pl.pallas_call(
        functools.partial(_mixers_kernel, layer=l, lam_init=lam_init),
        grid=(b, nq),
        in_specs=[pl.BlockSpec(memory_space=pltpu.SMEM),
                  qt(T_QA, A_W), qt(T_QB, B_W), qt(T_QC, C_W),
                  krow(K_KA, A_W), krow(K_KB, KV_B), krow(K_KS, KV_C), krow(K_KW, KV_C),
                  vt(T_VA, A_W), vt(T_VB, KV_B), vt(T_VS, KV_C), vt(T_VW, KV_C),
                  per_batch(kcmp), per_batch(vcmpt),
                  gate(G_GA, A_W), gate(G_GB, B_W), gate(G_GC, C_W),
                  pl.BlockSpec((NGT, TQ), lambda bi, qi: (0, bi * nq + qi)),
                  _layer(lq1, l), _layer(lk1, l), _layer(lq2, l), _layer(lk2, l), _layer(sub, l), const(ovt),
                  const(kx), const(qxa), const(qxb), const(qxc)],
        out_specs=pl.BlockSpec((None, TQ, D_MIX), lambda bi, qi: (bi, qi, 0)),
        out_shape=jax.ShapeDtypeStruct((b, s, D_MIX), BF16),
        scratch_shapes=[pltpu.VMEM((A_HEADS, 2 * LANES, 2 * TQ), BF16),
                        pltpu.VMEM((B_HEADS // 2, 2 * LANES, 2 * TQ), BF16),
                        pltpu.VMEM((C_HEADS // 2, 2 * LANES, 2 * TQ), BF16),
                        pltpu.VMEM((A_HEADS, V_ROWS, s), BF16),
                        pltpu.VMEM((B_KV, V_ROWS, s), BF16),
                        pltpu.VMEM((2, C_KV, V_ROWS, s), BF16),
                        pltpu.VMEM((TK, 2 * A_HEADS * TQ), BF16),
                        pltpu.VMEM((TK, B_HEADS * TQ), BF16),
                        pltpu.VMEM((TK, C_HEADS * TQ), BF16),
                        pltpu.VMEM((A_HEADS, V_ROWS, 2 * TQ), F32),
                        pltpu.VMEM((B_HEADS // 2, V_ROWS, 2 * TQ), F32),
                        pltpu.VMEM((2, C_HEADS // 2, V_ROWS, 2 * TQ), F32),
                        pltpu.VMEM((2 * A_HEADS, TQ), F32),
                        pltpu.VMEM((SUBLANES, TQ), F32),
                        pltpu.VMEM((2, C_HEADS, TQ), F32),
                        pltpu.VMEM((C_KV, s // TK, BF16_ROWS, TQ), BF16),
                        pltpu.VMEM((AHEAD, TK, 2 * TQ), F32),
                        pltpu.VMEM((TQ, D_MIX), F32)],
        compiler_params=_params("arbitrary", "arbitrary"),
        name="mixers",
    )(sinks, ht, ht, ht, hk3, hk3, hk3, hk3, ht, ht, ht, ht, kcmp, vcmpt, hg3, hg3, hg3, gt,
      lq1, lk1, lq2, lk2, sub, ovt, kx, qxa, qxb, qxc)


def kernel(x, norm_w, w_in, w_out, diff_lq1, diff_lk1, diff_lq2, diff_lk2, diff_subln, sinks,
           cmp_pe_k, cmp_pe_v, cmp_wk1, cmp_wk2, cmp_wv1, cmp_wv2, final_norm):
    b, s, d = x.shape
    depth = w_in.shape[0]
    assert s % TK == 0 and s // TK >= 3 and (b * s) % TM == 0 and d % LANES == 0

    w_rm, w_t = _prep_w_in(w_in)
    w_out_b = w_out.astype(BF16)
    pe2 = lambda pe: jnp.concatenate([pe, pe], axis=-1)
    w1 = lambda w: jnp.concatenate([w.reshape(depth, CMP_LEN, HEAD_DIM, CMP_HID)] * 2, axis=2).astype(BF16)
    zero2 = jnp.zeros((depth, CMP_HID, HEAD_DIM), F32)
    w2 = lambda w: jnp.stack([jnp.concatenate([w, zero2], axis=-1),
                              jnp.concatenate([zero2, w], axis=-1)], axis=1)
    pek, pev = pe2(cmp_pe_k), pe2(cmp_pe_v)
    wk1, wv1 = w1(cmp_wk1), w1(cmp_wv1)
    wk2 = w2(cmp_wk2).astype(BF16)
    wv2t = jnp.swapaxes(w2(cmp_wv2), 2, 3).astype(BF16)
    sub = jnp.broadcast_to(diff_subln[:, :, None], (depth, HEAD_DIM, TQ))
    fw = final_norm[None, :]
    nw = norm_w[:, None, :]
    lam_rows = [a[:, None, :] for a in (diff_lq1, diff_lk1, diff_lq2, diff_lk2)]
    x2 = x.reshape(b * s, d)
    y = None
    for l in range(depth):
        if y is None:
            hk, hg, ht, gt = _in_proj(l, x2, nw, w_rm, w_t)
        else:
            x2, hk, hg, ht, gt = _in_proj(l, x2, nw, w_rm, w_t, y, w_out_b)
        hk3 = hk.reshape(b, s, NK)
        hg3 = hg.reshape(b, s, NG)
        kcmp, vcmpt = _compress(l, hg3, pek, pev, wk1, wk2, wv1, wv2t)
        lam_init = 0.8 - 0.6 * math.exp(-0.3 * l)
        y = _mixers(l, sinks, hk3, hg3, ht, gt, kcmp, vcmpt, *lam_rows, sub, lam_init).reshape(b * s, D_MIX)
    return _out_proj(depth - 1, y, w_out_b, x2, fw).reshape(b, s, d)
```

```python
import functools
import math

import numpy as np
import jax
import jax.numpy as jnp
from jax import lax
from jax.experimental import pallas as pl
from jax.experimental.pallas import tpu as pltpu

F32 = jnp.float32
BF16 = jnp.bfloat16

HEAD_DIM = 64
RMS_EPS = 1e-6
NEG = -1e30
FORCE = 1e30

A_HEADS = 4
A_QK = HEAD_DIM // 2
B_HEADS = 4
B_KV = 2
B_WINDOW = 128
C_HEADS = 8
C_KV = 2
CMP_LEN = 32
CMP_STRIDE = 16
CMP_HID = 128
SLC_LEN = 64
SLC_TOPN = 8
C_WINDOW = 512

A_W = A_HEADS * HEAD_DIM
B_W = B_HEADS * HEAD_DIM
C_W = C_HEADS * HEAD_DIM
D_MIX = A_W + B_W + C_W
KV_B = B_KV * HEAD_DIM
KV_C = C_KV * HEAD_DIM
N_GATES = 3 * C_HEADS

LANES = 128
SUBLANES = 8
BF16_ROWS = 16
TQ = 128
TK = 256
TM = 512
AHEAD = 4
V_ROWS = HEAD_DIM + BF16_ROWS
VMEM_LIMIT = 48 * 1024 * 1024

LOG2E = math.log2(math.e)

_O = dict(qa=0, ka=256, va=512, ga=768, qb=1024, kb=1280, vb=1408, gb=1536, qc=1792,
          kc=2304, vc=2432, ks=2560, vs=2688, kw=2816, vw=2944, gl=3072, gc=3096)

K_KA, K_KB, K_KS, K_KW = 0, 256, 384, 512
NK = 640
G_GA, G_GB, G_GC, G_KC, G_VC = 0, 256, 512, 1024, 1152
NG = 1280
T_QA, T_QB, T_QC, T_VA, T_VB, T_VS, T_VW = 0, 256, 512, 1024, 1280, 1408, 1536
NT = 1664
NGT = 32


def _slopes(n):
    return [float(2.0 ** (-8.0 * h / n)) * LOG2E for h in range(1, n + 1)]


_SLOPE_A = _slopes(A_HEADS)
_SLOPE_B = _slopes(B_HEADS)
_SLOPE_C = _slopes(C_HEADS)


def _params(*sem):
    return pltpu.CompilerParams(dimension_semantics=sem, vmem_limit_bytes=VMEM_LIMIT)


def _dot_nt(a, b):
    return lax.dot_general(a, b, (((1,), (1,)), ((), ())), preferred_element_type=F32)


def _dot(a, b):
    return jnp.dot(a, b, preferred_element_type=F32)


def _silu(g):
    return g * (1.0 / (1.0 + jnp.exp(-g)))


def _prep_w_in(w_in):
    depth, d, _ = w_in.shape
    col = lambda name, width: w_in[:, :, _O[name]:_O[name] + width]
    w_rm = jnp.concatenate([col('ka', A_W), col('kb', KV_B), col('ks', KV_C), col('kw', KV_C),
                            col('ga', A_W), col('gb', B_W), col('gc', C_W), col('kc', KV_C), col('vc', KV_C)],
                           axis=2).astype(BF16)
    gl = jnp.swapaxes(col('gl', N_GATES).reshape(depth, d, C_HEADS, 3), 2, 3).reshape(depth, d, N_GATES)
    gl = jnp.concatenate([gl, jnp.zeros((depth, d, NGT - N_GATES), w_in.dtype)], axis=2)
    cols_t = jnp.concatenate([col('qa', A_W) * (A_QK ** -0.5 * LOG2E),
                              col('qb', B_W) * (HEAD_DIM ** -0.5 * LOG2E),
                              col('qc', C_W) * (HEAD_DIM ** -0.5 * LOG2E),
                              col('va', A_W), col('vb', KV_B), col('vs', KV_C), col('vw', KV_C), gl], axis=2)
    w_t = jnp.swapaxes(cols_t.astype(BF16), 1, 2)
    return w_rm, w_t


EXT_ROWS = 2 * BF16_ROWS
POS_SPLIT = 256


def _bf16_pieces(x, n=3):
    out, rest = [], np.float32(x)
    for _ in range(n):
        bits = np.array([rest], np.float32).view(np.uint32)
        bits = (bits + 0x7FFF + ((bits >> 16) & 1)) & np.uint32(0xFFFF0000)
        piece = bits.view(np.float32)[0]
        out.append(piece)
        rest = np.float32(rest - piece)
    return out


def _key_ext(s):
    pos = np.arange(s)
    ext = np.zeros((s, 2, LANES), np.float32)
    for j in range(3):
        ext[:, :, j] = (pos // POS_SPLIT)[:, None]
        ext[:, :, 3 + j] = (pos % POS_SPLIT)[:, None]
    blk = (pos % TK) // SLC_LEN
    ext[pos, 1, BF16_ROWS + blk] = 1.0
    return jnp.asarray(ext.reshape(s, 2 * LANES), dtype=BF16)


def _query_ext(slopes):
    ext = np.zeros((len(slopes) // 2, EXT_ROWS, 2, TQ), np.float32)
    for c, slope in enumerate(slopes):
        for j, piece in enumerate(_bf16_pieces(slope)):
            ext[c // 2, j, c % 2, :] = piece * POS_SPLIT
            ext[c // 2, 3 + j, c % 2, :] = piece
    return jnp.asarray(ext.reshape(len(slopes) // 2, EXT_ROWS, 2 * TQ), dtype=BF16)


def _in_proj_kernel(*refs, fused):
    if fused:
        y_ref, wo_ref, x_ref, nw_ref, w_ref, wt_ref, xo_ref, hk_ref, hg_ref, ht_ref, gt_ref = refs
        x = x_ref[...] + _dot(y_ref[...], wo_ref[...])
        xo_ref[...] = x
    else:
        x_ref, nw_ref, w_ref, wt_ref, hk_ref, hg_ref, ht_ref, gt_ref = refs
        x = x_ref[...]
    ms = jnp.mean(x * x, axis=-1, keepdims=True)
    xn = ((x * lax.rsqrt(ms + RMS_EPS)) * nw_ref[...]).astype(BF16)
    step = 256
    for c0 in range(0, NK, step):
        c1 = min(c0 + step, NK)
        hk_ref[:, c0:c1] = _dot(xn, w_ref[:, c0:c1]).astype(BF16)
    for c0 in range(0, NG, step):
        c1 = min(c0 + step, NG)
        g = _dot(xn, w_ref[:, NK + c0:NK + c1])
        hg_ref[:, c0:c1] = _silu(g) if c1 <= G_KC else g
    for r0 in range(0, NT, step):
        r1 = min(r0 + step, NT)
        ht_ref[r0:r1, :] = _dot_nt(wt_ref[r0:r1, :], xn).astype(BF16)
    gt_ref[...] = 1.0 / (1.0 + jnp.exp(-_dot_nt(wt_ref[NT:NT + NGT, :], xn)))


def _layer(a, l):
    return pl.BlockSpec((None,) + a.shape[1:], lambda *_: (l,) + (0,) * (a.ndim - 1))


def _in_proj(l, x2, nw, w_rm, w_t, y=None, w_out=None):
    n, d = x2.shape
    fused = y is not None
    row = lambda cols: pl.BlockSpec((TM, cols), lambda i: (i, 0))
    in_specs = [row(d), _layer(nw, l), _layer(w_rm, l), _layer(w_t, l)]
    out_specs = [row(NK), row(NG), pl.BlockSpec((NT, TM), lambda i: (0, i)), pl.BlockSpec((NGT, TM), lambda i: (0, i))]
    out_shape = [jax.ShapeDtypeStruct((n, NK), BF16), jax.ShapeDtypeStruct((n, NG), F32),
                 jax.ShapeDtypeStruct((NT, n), BF16), jax.ShapeDtypeStruct((NGT, n), F32)]
    args = (x2, nw, w_rm, w_t)
    if fused:
        in_specs = [row(D_MIX), _layer(w_out, l - 1)] + in_specs
        out_specs = [row(d)] + out_specs
        out_shape = [jax.ShapeDtypeStruct((n, d), F32)] + out_shape
        args = (y, w_out) + args
    return pl.pallas_call(
        functools.partial(_in_proj_kernel, fused=fused),
        grid=(n // TM,),
        in_specs=in_specs, out_specs=out_specs, out_shape=out_shape,
        compiler_params=_params("arbitrary"),
        name="out_in_proj" if fused else "in_proj",
    )(*args)


def _out_proj_kernel(y_ref, w_ref, x_ref, fw_ref, o_ref):
    acc = x_ref[...] + _dot(y_ref[...], w_ref[...])
    ms = jnp.mean(acc * acc, axis=-1, keepdims=True)
    o_ref[...] = (acc * lax.rsqrt(ms + RMS_EPS)) * fw_ref[...]


def _out_proj(l, y, w_out, x2, fw):
    n, d = x2.shape
    return pl.pallas_call(
        _out_proj_kernel,
        grid=(n // TM,),
        in_specs=[pl.BlockSpec((TM, D_MIX), lambda i: (i, 0)),
                  _layer(w_out, l),
                  pl.BlockSpec((TM, d), lambda i: (i, 0)),
                  pl.BlockSpec((1, d), lambda i: (0, 0))],
        out_specs=pl.BlockSpec((TM, d), lambda i: (i, 0)),
        out_shape=jax.ShapeDtypeStruct((n, d), F32),
        compiler_params=_params("arbitrary"),
        name="out_proj",
    )(y, w_out, x2, fw)


def _compress_kernel(kc_ref, vc_ref, pek_ref, pev_ref, wk1_ref, wk2_ref, wv1_ref, wv2_ref,
                     kcmp_ref, vcmpt_ref):
    nch = kc_ref.shape[0] // CMP_STRIDE
    left = lax.broadcasted_iota(jnp.int32, (nch, LANES), 1) < HEAD_DIM

    def one(x_ref, pe_ref, w1_ref, w2_ref, out_ref, transposed):
        u = [jnp.zeros((nch, CMP_HID), F32) for _ in range(C_KV)]
        v = [jnp.zeros((nch, CMP_HID), F32) for _ in range(C_KV)]
        for l in range(CMP_STRIDE):
            xl = x_ref[pl.ds(l, nch, stride=CMP_STRIDE), :]
            pe_u = pe_ref[l:l + 1, :]
            pe_v = pe_ref[CMP_STRIDE + l:CMP_STRIDE + l + 1, :]
            w_u = w1_ref[l]
            w_v = w1_ref[CMP_STRIDE + l]
            for g in range(C_KV):
                keep = left if g == 0 else jnp.logical_not(left)
                u[g] = u[g] + _dot(jnp.where(keep, xl + pe_u, 0.0).astype(BF16), w_u)
                v[g] = v[g] + _dot(jnp.where(keep, xl + pe_v, 0.0).astype(BF16), w_v)
        acc = jnp.zeros((LANES, nch) if transposed else (nch, LANES), F32)
        for g in range(C_KV):
            hid = u[g] + pltpu.roll(v[g], nch - 1, 0)
            act = _silu(hid).astype(BF16)
            acc = acc + (_dot_nt(w2_ref[g], act) if transposed else _dot(act, w2_ref[g]))
        out_ref[...] = acc.astype(BF16)

    one(kc_ref, pek_ref, wk1_ref, wk2_ref, kcmp_ref, False)
    one(vc_ref, pev_ref, wv1_ref, wv2_ref, vcmpt_ref, True)


def _compress(l, hg3, pek, pev, wk1, wk2, wv1, wv2t):
    b, s, _ = hg3.shape
    nch = s // CMP_STRIDE
    full = lambda a: _layer(a, l)
    return pl.pallas_call(
        _compress_kernel,
        grid=(b,),
        in_specs=[pl.BlockSpec((None, s, LANES), lambda i: (i, 0, G_KC // LANES)),
                  pl.BlockSpec((None, s, LANES), lambda i: (i, 0, G_VC // LANES)),
                  full(pek), full(pev), full(wk1), full(wk2), full(wv1), full(wv2t)],
        out_specs=[pl.BlockSpec((None, nch, LANES), lambda i: (i, 0, 0)),
                   pl.BlockSpec((None, LANES, nch), lambda i: (i, 0, 0))],
        out_shape=[jax.ShapeDtypeStruct((b, nch, LANES), BF16), jax.ShapeDtypeStruct((b, LANES, nch), BF16)],
        compiler_params=_params("arbitrary"),
        name="compress",
    )(hg3, hg3, pek, pev, wk1, wk2, wv1, wv2t)


def _fill_value_rows(vaug_ref, vt_ref, n_heads):
    s = vt_ref.shape[1]
    for h in range(n_heads):
        vaug_ref[h, 0:HEAD_DIM, :] = vt_ref[h * HEAD_DIM:(h + 1) * HEAD_DIM, :]
        vaug_ref[h, HEAD_DIM:V_ROWS, :] = jnp.ones((BF16_ROWS, s), BF16)


def _stack_gqa_q(qt_ref, qst_ref, heads_per_group):
    zero = jnp.zeros((HEAD_DIM, TQ), BF16)
    for p in range(qst_ref.shape[0]):
        cols = []
        for h in (2 * p, 2 * p + 1):
            q = qt_ref[h * HEAD_DIM:(h + 1) * HEAD_DIM, :]
            cols.append(jnp.concatenate([q, zero] if h // heads_per_group == 0 else [zero, q], axis=0))
        qst_ref[p, 0:LANES, :] = jnp.concatenate(cols, axis=1)


def _set_ext_rows(qst_ref, qx_ref):
    n = qst_ref.shape[0]
    qst_ref[:, LANES:LANES + EXT_ROWS, :] = qx_ref[...]
    qst_ref[:, LANES + EXT_ROWS:, :] = jnp.zeros((n, LANES - EXT_ROWS, 2 * TQ), BF16)


def _ext_keys(k_chunk, kx_ref, s0, selected):
    c0 = LANES if selected else 0
    return jnp.concatenate([k_chunk, kx_ref[pl.ds(s0, k_chunk.shape[0]), c0:c0 + LANES]], axis=1)


def _flash_pair(st, c0, madd, m_ref, e_ref):
    rows = st.shape[0]
    alphas = []
    for j in range(2):
        c = c0 + j
        u = st[:, j * TQ:(j + 1) * TQ]
        if madd is not None:
            u = u + madd
        m_old = m_ref[c:c + 1, :]
        m_new = jnp.maximum(m_old, jnp.max(u, axis=0, keepdims=True))
        e_ref[0:rows, c * TQ:(c + 1) * TQ] = jnp.exp2(u - m_new).astype(BF16)
        alphas.append(jnp.exp2(m_old - m_new))
        m_ref[c:c + 1, :] = m_new
    return alphas


def _accumulate(acc_ref, idx, vaug_chunk, e_ref, c0, alphas):
    a = jnp.concatenate(alphas, axis=1)
    rows = vaug_chunk.shape[1]
    acc_ref[idx] = acc_ref[idx] * a + _dot(vaug_chunk, e_ref[0:rows, c0 * TQ:(c0 + 2) * TQ])


def _pipelined(items, st_ref=None, lookahead=()):
    if st_ref is None:
        queue = [item[0]() for item in items[:AHEAD]]
    else:
        assert len(items) >= AHEAD and len(lookahead) in (0, AHEAD)
        queue = [functools.partial(lambda j: st_ref[j], j) for j in range(AHEAD)]
    todo = [item[0] for item in items[AHEAD:]] + [item[0] for item in lookahead]
    pending = None
    for _, softmax, values in items:
        if todo:
            queue.append(todo.pop(0)())
        if pending is not None:
            pending[0](pending[1])
        st = queue.pop(0)
        pending = (values, softmax(st() if callable(st) else st))
    pending[0](pending[1])
    for j, st in enumerate(queue):
        st_ref[j] = st


def _normalized(acc, lane0):
    return acc[0:HEAD_DIM, lane0:lane0 + TQ] / acc[HEAD_DIM:HEAD_DIM + 1, lane0:lane0 + TQ]


def _split3(x):
    hi = x.astype(BF16)
    r1 = x - hi.astype(F32)
    mid = r1.astype(BF16)
    lo = (r1 - mid.astype(F32)).astype(BF16)
    return hi, mid, lo


def _mixers_kernel(sink_ref,
                   qa_ref, qb_ref, qc_ref, ka_ref, kb_ref, ks_ref, kw_ref, va_ref, vb_ref, vs_ref, vw_ref,
                   kcmp_ref, vcmpt_ref, ga_ref, gb_ref, gc_ref, glt_ref,
                   lq1_ref, lk1_ref, lq2_ref, lk2_ref, sub_ref, ovt_ref, kx_ref, qxa_ref, qxb_ref, qxc_ref,
                   y_ref,
                   qsta_ref, qstb_ref, qstc_ref, vauga_ref, vaugb_ref, vaugc_ref,
                   ea_ref, eb_ref, ec_ref, acca_ref, accb_ref, accc_ref, ma_ref, mb_ref, mc_ref,
                   msel_ref, st_ref, *, layer, lam_init):
    qi = pl.program_id(1)
    s_len = ka_ref.shape[0]
    n_slc = s_len // SLC_LEN
    ncmp = kcmp_ref.shape[0]
    blocks_per_chunk = TK // SLC_LEN
    cpg = C_HEADS // C_KV

    @pl.when(qi == 0)
    def _():
        _fill_value_rows(vauga_ref, va_ref, A_HEADS)
        _fill_value_rows(vaugb_ref, vb_ref, B_KV)
        _fill_value_rows(vaugc_ref.at[0], vs_ref, C_KV)
        _fill_value_rows(vaugc_ref.at[1], vw_ref, C_KV)
        _set_ext_rows(qsta_ref, qxa_ref)
        _set_ext_rows(qstb_ref, qxb_ref)
        _set_ext_rows(qstc_ref, qxc_ref)

    sub32 = lax.broadcasted_iota(jnp.int32, (LANES, TQ), 0) // A_QK
    for h in range(A_HEADS):
        blk = h * HEAD_DIM // LANES
        q = qa_ref[blk * LANES:(blk + 1) * LANES, :]
        zero = jnp.zeros_like(q)
        first = 2 * (h % (LANES // HEAD_DIM))
        qsta_ref[h, 0:LANES, :] = jnp.concatenate(
            [jnp.where(sub32 == first, q, zero), jnp.where(sub32 == first + 1, q, zero)], axis=1)
    _stack_gqa_q(qb_ref, qstb_ref, B_HEADS // B_KV)
    _stack_gqa_q(qc_ref, qstc_ref, cpg)
    acca_ref[...] = jnp.zeros(acca_ref.shape, F32)
    accc_ref[...] = jnp.zeros(accc_ref.shape, F32)
    t0 = qi * TQ
    den_rows = lax.broadcasted_iota(jnp.int32, (V_ROWS, 2 * TQ), 0) >= HEAD_DIM
    for g in range(B_KV):
        accb_ref[g] = jnp.where(den_rows, 1.0, 0.0)
    t_abs = (t0 + lax.broadcasted_iota(jnp.int32, (1, TQ), 1)).astype(F32)
    for c in range(B_HEADS):
        mb_ref[c:c + 1, :] = sink_ref[layer, c] * LOG2E + _SLOPE_B[c] * t_abs
    ma_ref[...] = jnp.full(ma_ref.shape, NEG, F32)
    mc_ref[...] = jnp.full(mc_ref.shape, NEG, F32)
    row_i = lax.broadcasted_iota(jnp.int32, (TK, TQ), 0)
    row_f = row_i.astype(F32)
    kmq = (row_i - lax.broadcasted_iota(jnp.int32, (TK, TQ), 1)).astype(F32)

    def start(ch):
        return pl.multiple_of(ch * TK, TK)

    masks = {}

    def cached(tag, build):
        if tag not in masks:
            masks[tag] = build()
        return masks[tag]

    def causal_mask(tag, s0):
        return cached(tag, lambda: jnp.where(kmq + (s0 - t0).astype(F32) <= 0.0, 0.0, NEG))

    def make_item(keys, weights, rows, madd, m_ref, e_ref, c0, acc_ref, idx, values):
        def scores():
            return _dot(keys(), weights())

        def softmax(st):
            return _flash_pair(st[0:rows], c0, madd(), m_ref, e_ref)

        def accumulate(alphas):
            _accumulate(acc_ref, idx, values(), e_ref, c0, alphas)

        return scores, softmax, accumulate

    def item_a(tag, ch, diagonal, h):
        blk = h * HEAD_DIM // LANES
        return make_item(
            lambda: _ext_keys(ka_ref[pl.ds(start(ch), TK), blk * LANES:(blk + 1) * LANES], kx_ref, start(ch), False),
            lambda: qsta_ref[h], TK,
            lambda: causal_mask(tag, start(ch)) if diagonal else None,
            ma_ref, ea_ref, 2 * h, acca_ref, h, lambda: vauga_ref[h, :, pl.ds(start(ch), TK)])

    def item_s(tag, ch, diagonal, p):
        g = 2 * p // cpg

        def weights():
            rows = msel_ref[g, ch]
            qstc_ref[p, LANES + BF16_ROWS:LANES + EXT_ROWS, :] = jnp.concatenate([rows, rows], axis=1)
            return qstc_ref[p]

        return make_item(
            lambda: _ext_keys(ks_ref[pl.ds(start(ch), TK), :], kx_ref, start(ch), True), weights, TK,
            lambda: causal_mask(tag, start(ch)) if diagonal else None,
            mc_ref.at[0], ec_ref, 2 * p, accc_ref.at[0], p, lambda: vaugc_ref[0, g, :, pl.ds(start(ch), TK)])

    def item_w(r, p):
        g = 2 * p // cpg
        n_full = C_WINDOW // TK
        rows = TK if r < n_full else TQ
        nominal = t0 - C_WINDOW + r * TK
        s0 = pl.multiple_of(jnp.maximum(nominal, 0), TQ)

        def madd():
            if r == n_full:
                return cached('own', lambda: jnp.where(kmq[0:rows] <= 0.0, 0.0, NEG))

            def build():
                ok = row_f < (nominal + TK - s0).astype(F32)
                if r == 0:
                    ok = jnp.logical_and(ok, kmq + (s0 - t0).astype(F32) > -float(C_WINDOW))
                return jnp.where(ok, 0.0, NEG)
            return cached(('win', r), build)

        return make_item(
            lambda: _ext_keys(kw_ref[pl.ds(s0, rows), :], kx_ref, s0, False), lambda: qstc_ref[p], rows, madd,
            mc_ref.at[1], ec_ref, 2 * p, accc_ref.at[1], p, lambda: vaugc_ref[1, g, :, pl.ds(s0, rows)])

    def item_b(g):
        s0 = pl.multiple_of(jnp.maximum(t0 - B_WINDOW, 0), LANES)

        def madd():
            nd = kmq + (s0 - t0).astype(F32)
            return jnp.where(jnp.logical_and(nd <= 0.0, nd > -float(B_WINDOW)), 0.0, NEG)

        return make_item(
            lambda: _ext_keys(kb_ref[pl.ds(s0, TK), :], kx_ref, s0, False), lambda: qstb_ref[g], TK, madd,
            mb_ref, eb_ref, 2 * g, accb_ref, g, lambda: vaugb_ref[g, :, pl.ds(s0, TK)])

    def causal_items(tag, ch, diagonal):
        return ([item_a(tag, ch, diagonal, h) for h in range(A_HEADS)]
                + [item_s(tag, ch, diagonal, p) for p in range(C_HEADS // 2)])

    last = (t0 + TQ - 1) // TK
    first = causal_items('first', 0, False)[:AHEAD]
    for j, item in enumerate(first[:A_HEADS]):
        st_ref[j] = item[0]()

    p_cmp = []
    tpos = t0 + lax.broadcasted_iota(jnp.int32, (ncmp, TQ), 1)
    cend = lax.broadcasted_iota(jnp.int32, (ncmp, TQ), 0) * CMP_STRIDE + (CMP_LEN - 1)
    dist_c = (tpos - cend).astype(F32)
    valid_c = dist_c >= 0.0
    for p in range(C_HEADS // 2):
        st = _dot(kcmp_ref[...], qstc_ref[p, 0:LANES, :])
        for j in range(2):
            c = 2 * p + j
            sc = jnp.where(valid_c, st[:, j * TQ:(j + 1) * TQ] - _SLOPE_C[c] * dist_c, NEG)
            mx = jnp.max(sc, axis=0, keepdims=True)
            e = jnp.where(valid_c, jnp.exp2(sc - mx), 0.0)
            den = jnp.sum(e, axis=0, keepdims=True)
            p_cmp.append(e / jnp.where(den > 0.0, den, 1.0))
    o_cmp = [_dot(vcmpt_ref[g * HEAD_DIM:(g + 1) * HEAD_DIM, :],
                  jnp.concatenate(p_cmp[g * cpg:(g + 1) * cpg], axis=1).astype(BF16))
             for g in range(C_KV)]

    jidx = lax.broadcasted_iota(jnp.int32, (n_slc, TQ), 0)
    tl = t0 + lax.broadcasted_iota(jnp.int32, (n_slc, TQ), 1)
    cur = tl // SLC_LEN
    causal_b = jidx * SLC_LEN <= tl
    forced = jnp.logical_or(jidx == 0, jnp.logical_or(jidx == cur, jidx == cur - 1))
    for g in range(C_KV):
        psum = p_cmp[g * cpg]
        for i in range(1, cpg):
            psum = psum + p_cmp[g * cpg + i]
        imp = jnp.zeros((n_slc, TQ), F32)
        for part in _split3(psum):
            imp = imp + _dot(ovt_ref[...], part)
        score = jnp.where(causal_b, jnp.where(forced, FORCE, imp), NEG)
        rank = jnp.zeros((n_slc, TQ), F32)
        for jp in range(n_slc):
            row = score[jp:jp + 1, :]
            tie = jnp.where(jidx > jp, 1.0, 0.0)
            rank = rank + jnp.where(row > score, 1.0, jnp.where(row == score, tie, 0.0))
        msel = jnp.where(rank < float(SLC_TOPN), 0.0, NEG)
        pad = jnp.zeros((BF16_ROWS - blocks_per_chunk, TQ), F32)
        for ch in range(n_slc // blocks_per_chunk):
            rows = msel[ch * blocks_per_chunk:(ch + 1) * blocks_per_chunk, :]
            msel_ref[g, ch] = jnp.concatenate([rows, pad], axis=0).astype(BF16)

    def two_chunks(i, carry):
        masks.clear()
        _pipelined(causal_items(0, 2 * i, False) + causal_items(1, 2 * i + 1, False), st_ref,
                   causal_items(2, 2 * i + 2, False)[:AHEAD])
        masks.clear()
        return carry

    lax.fori_loop(0, last // 2, two_chunks, 0)

    def tail_items():
        return ([item_w(r, p) for r in range(C_WINDOW // TK + 1) for p in range(C_HEADS // 2)]
                + [item_b(g) for g in range(B_KV)])

    @pl.when(last % 2 == 1)
    def _():
        masks.clear()
        _pipelined(causal_items(0, last - 1, False) + causal_items(1, last, True) + tail_items(), st_ref)
        masks.clear()

    @pl.when(last % 2 == 0)
    def _():
        masks.clear()
        _pipelined(causal_items(1, last, True) + tail_items(), st_ref)
        masks.clear()

    def emit(col0, outs, g_ref):
        for i in range(len(outs) // 2):
            o = jnp.transpose(jnp.concatenate(outs[2 * i:2 * i + 2], axis=0))
            gate = g_ref[:, i * LANES:(i + 1) * LANES]
            y_ref[:, col0 + i * LANES:col0 + (i + 1) * LANES] = (o * gate).astype(BF16)

    lam = (jnp.exp(jnp.sum(lq1_ref[...] * lk1_ref[...], axis=1, keepdims=True))
           - jnp.exp(jnp.sum(lq2_ref[...] * lk2_ref[...], axis=1, keepdims=True)) + lam_init)
    outs = []
    for h in range(A_HEADS):
        acc = acca_ref[h]
        d = _normalized(acc, 0) - lam * _normalized(acc, TQ)
        ms = jnp.mean(d * d, axis=0, keepdims=True)
        outs.append(((d * lax.rsqrt(ms + RMS_EPS)) * sub_ref[...]) * (1.0 - lam_init))
    emit(0, outs, ga_ref)

    outs = []
    for c in range(B_HEADS):
        outs.append(_normalized(accb_ref[c // 2], (c % 2) * TQ))
    emit(A_W, outs, gb_ref)

    sig = glt_ref[...]
    outs = []
    for c in range(C_HEADS):
        g, i = c // cpg, c % cpg
        gate = lambda br: sig[br * C_HEADS + c:br * C_HEADS + c + 1, :]
        lane0 = (c % 2) * TQ
        outs.append(gate(0) * o_cmp[g][:, i * TQ:(i + 1) * TQ]
                    + gate(1) * _normalized(accc_ref[0, c // 2], lane0)
                    + gate(2) * _normalized(accc_ref[1, c // 2], lane0))
    emit(A_W + B_W, outs, gc_ref)


def _overlap_t(ncmp_pad, n_slc):
    cs = np.arange(ncmp_pad)[None, :] * CMP_STRIDE
    ss = np.arange(n_slc)[:, None] * SLC_LEN
    ov = np.minimum(cs + CMP_LEN, ss + SLC_LEN) - np.maximum(cs, ss)
    ov = np.maximum(ov, 0) / CMP_LEN
    ov[:, ncmp_pad - 1] = 0.0
    return ov.astype(np.float32)


def _mixers(l, sinks, hk3, hg3, ht, gt, kcmp, vcmpt, lq1, lk1, lq2, lk2, sub, lam_init):
    b, s, _ = hk3.shape
    assert B_WINDOW + TQ <= TK and B_HEADS <= SUBLANES and C_WINDOW % TK == 0 and TQ % BF16_ROWS == 0
    assert AHEAD <= A_HEADS
    n_slc = s // SLC_LEN
    ncmp = kcmp.shape[1]
    nq = s // TQ
    ovt = jnp.asarray(_overlap_t(ncmp, n_slc), dtype=BF16)
    kx = _key_ext(s)
    qxa = _query_ext([_SLOPE_A[c // 2] for c in range(2 * A_HEADS)])
    qxb = _query_ext(_SLOPE_B)
    qxc = _query_ext(_SLOPE_C)
    const = lambda a: pl.BlockSpec(a.shape, lambda bi, qi: (0,) * a.ndim)
    qt = lambda off, rows: pl.BlockSpec((rows, TQ), lambda bi, qi: (off // rows, bi * nq + qi))
    vt = lambda off, rows: pl.BlockSpec((rows, s), lambda bi, qi: (off // rows, bi))
    krow = lambda off, cols: pl.BlockSpec((None, s, cols), lambda bi, qi: (bi, 0, off // cols))
    gate = lambda off, cols: pl.BlockSpec((None, TQ, cols), lambda bi, qi: (bi, qi, off // cols))
    per_batch = lambda a: pl.BlockSpec((None,) + a.shape[1:], lambda bi, qi: (bi, 0, 0))
    return pl.pallas_call(
        functools.partial(_mixers_kernel, layer=l, lam_init=lam_init),
        grid=(b, nq),
        in_specs=[pl.BlockSpec(memory_space=pltpu.SMEM),
                  qt(T_QA, A_W), qt(T_QB, B_W), qt(T_QC, C_W),
                  krow(K_KA, A_W), krow(K_KB, KV_B), krow(K_KS, KV_C), krow(K_KW, KV_C),
                  vt(T_VA, A_W), vt(T_VB, KV_B), vt(T_VS, KV_C), vt(T_VW, KV_C),
                  per_batch(kcmp), per_batch(vcmpt),
                  gate(G_GA, A_W), gate(G_GB, B_W), gate(G_GC, C_W),
                  pl.BlockSpec((NGT, TQ), lambda bi, qi: (0, bi * nq + qi)),
                  _layer(lq1, l), _layer(lk1, l), _layer(lq2, l), _layer(lk2, l), _layer(sub, l), const(ovt),
                  const(kx), const(qxa), const(qxb), const(qxc)],
        out_specs=pl.BlockSpec((None, TQ, D_MIX), lambda bi, qi: (bi, qi, 0)),
        out_shape=jax.ShapeDtypeStruct((b, s, D_MIX), BF16),
        scratch_shapes=[pltpu.VMEM((A_HEADS, 2 * LANES, 2 * TQ), BF16),
                        pltpu.VMEM((B_HEADS // 2, 2 * LANES, 2 * TQ), BF16),
                        pltpu.VMEM((C_HEADS // 2, 2 * LANES, 2 * TQ), BF16),
                        pltpu.VMEM((A_HEADS, V_ROWS, s), BF16),
                        pltpu.VMEM((B_KV, V_ROWS, s), BF16),
                        pltpu.VMEM((2, C_KV, V_ROWS, s), BF16),
                        pltpu.VMEM((TK, 2 * A_HEADS * TQ), BF16),
                        pltpu.VMEM((TK, B_HEADS * TQ), BF16),
                        pltpu.VMEM((TK, C_HEADS * TQ), BF16),
                        pltpu.VMEM((A_HEADS, V_ROWS, 2 * TQ), F32),
                        pltpu.VMEM((B_HEADS // 2, V_ROWS, 2 * TQ), F32),
                        pltpu.VMEM((2, C_HEADS // 2, V_ROWS, 2 * TQ), F32),
                        pltpu.VMEM((2 * A_HEADS, TQ), F32),
                        pltpu.VMEM((SUBLANES, TQ), F32),
                        pltpu.VMEM((2, C_HEADS, TQ), F32),
                        pltpu.VMEM((C_KV, s // TK, BF16_ROWS, TQ), BF16),
                        pltpu.VMEM((AHEAD, TK, 2 * TQ), F32)],
        compiler_params=_params("arbitrary", "arbitrary"),
        name="mixers",
    )(sinks, ht, ht, ht, hk3, hk3, hk3, hk3, ht, ht, ht, ht, kcmp, vcmpt, hg3, hg3, hg3, gt,
      lq1, lk1, lq2, lk2, sub, ovt, kx, qxa, qxb, qxc)


def kernel(x, norm_w, w_in, w_out, diff_lq1, diff_lk1, diff_lq2, diff_lk2, diff_subln, sinks,
           cmp_pe_k, cmp_pe_v, cmp_wk1, cmp_wk2, cmp_wv1, cmp_wv2, final_norm):
    b, s, d = x.shape
    depth = w_in.shape[0]
    assert s % TK == 0 and s // TK >= 3 and (b * s) % TM == 0 and d % LANES == 0

    w_rm, w_t = _prep_w_in(w_in)
    w_out_b = w_out.astype(BF16)
    pe2 = lambda pe: jnp.concatenate([pe, pe], axis=-1)
    w1 = lambda w: jnp.concatenate([w.reshape(depth, CMP_LEN, HEAD_DIM, CMP_HID)] * 2, axis=2).astype(BF16)
    zero2 = jnp.zeros((depth, CMP_HID, HEAD_DIM), F32)
    w2 = lambda w: jnp.stack([jnp.concatenate([w, zero2], axis=-1),
                              jnp.concatenate([zero2, w], axis=-1)], axis=1)
    pek, pev = pe2(cmp_pe_k), pe2(cmp_pe_v)
    wk1, wv1 = w1(cmp_wk1), w1(cmp_wv1)
    wk2 = w2(cmp_wk2).astype(BF16)
    wv2t = jnp.swapaxes(w2(cmp_wv2), 2, 3).astype(BF16)
    sub = jnp.broadcast_to(diff_subln[:, :, None], (depth, HEAD_DIM, TQ))
    fw = final_norm[None, :]
    nw = norm_w[:, None, :]
    lam_rows = [a[:, None, :] for a in (diff_lq1, diff_lk1, diff_lq2, diff_lk2)]
    x2 = x.reshape(b * s, d)
    y = None
    for l in range(depth):
        if y is None:
            hk, hg, ht, gt = _in_proj(l, x2, nw, w_rm, w_t)
        else:
            x2, hk, hg, ht, gt = _in_proj(l, x2, nw, w_rm, w_t, y, w_out_b)
        hk3 = hk.reshape(b, s, NK)
        hg3 = hg.reshape(b, s, NG)
        kcmp, vcmpt = _compress(l, hg3, pek, pev, wk1, wk2, wv1, wv2t)
        lam_init = 0.8 - 0.6 * math.exp(-0.3 * l)
        y = _mixers(l, sinks, hk3, hg3, ht, gt, kcmp, vcmpt, *lam_rows, sub, lam_init).reshape(b * s, D_MIX)
    return _out_proj(depth - 1, y, w_out_b, x2, fw).reshape(b, s, d)
```

```python
import functools
import math

import numpy as np
import jax
import jax.numpy as jnp
from jax import lax
from jax.experimental import pallas as pl
from jax.experimental.pallas import tpu as pltpu

F32 = jnp.float32
BF16 = jnp.bfloat16

HEAD_DIM = 64
RMS_EPS = 1e-6
NEG = -1e30
FORCE = 1e30

A_HEADS = 4
A_QK = HEAD_DIM // 2
B_HEADS = 4
B_KV = 2
B_WINDOW = 128
C_HEADS = 8
C_KV = 2
CMP_LEN = 32
CMP_STRIDE = 16
CMP_HID = 128
SLC_LEN = 64
SLC_TOPN = 8
C_WINDOW = 512

A_W = A_HEADS * HEAD_DIM
B_W = B_HEADS * HEAD_DIM
C_W = C_HEADS * HEAD_DIM
D_MIX = A_W + B_W + C_W
KV_B = B_KV * HEAD_DIM
KV_C = C_KV * HEAD_DIM
N_GATES = 3 * C_HEADS

LANES = 128
SUBLANES = 8
BF16_ROWS = 16
TQ = 128
TK = 256
TM = 512
AHEAD = 4
V_ROWS = HEAD_DIM + BF16_ROWS
VMEM_LIMIT = 48 * 1024 * 1024

LOG2E = math.log2(math.e)

_O = dict(qa=0, ka=256, va=512, ga=768, qb=1024, kb=1280, vb=1408, gb=1536, qc=1792,
          kc=2304, vc=2432, ks=2560, vs=2688, kw=2816, vw=2944, gl=3072, gc=3096)

K_KA, K_KB, K_KS, K_KW = 0, 256, 384, 512
NK = 640
G_GA, G_GB, G_GC, G_KC, G_VC = 0, 256, 512, 1024, 1152
NG = 1280
T_QA, T_QB, T_QC, T_VA, T_VB, T_VS, T_VW = 0, 256, 512, 1024, 1280, 1408, 1536
NT = 1664
NGT = 32


def _slopes(n):
    return [float(2.0 ** (-8.0 * h / n)) * LOG2E for h in range(1, n + 1)]


_SLOPE_A = _slopes(A_HEADS)
_SLOPE_B = _slopes(B_HEADS)
_SLOPE_C = _slopes(C_HEADS)


def _params(*sem):
    return pltpu.CompilerParams(dimension_semantics=sem, vmem_limit_bytes=VMEM_LIMIT)


def _dot_nt(a, b):
    return lax.dot_general(a, b, (((1,), (1,)), ((), ())), preferred_element_type=F32)


def _dot(a, b):
    return jnp.dot(a, b, preferred_element_type=F32)


def _silu(g):
    return g * (1.0 / (1.0 + jnp.exp(-g)))


def _prep_w_in(w_in):
    depth, d, _ = w_in.shape
    col = lambda name, width: w_in[:, :, _O[name]:_O[name] + width]
    w_rm = jnp.concatenate([col('ka', A_W), col('kb', KV_B), col('ks', KV_C), col('kw', KV_C),
                            col('ga', A_W), col('gb', B_W), col('gc', C_W), col('kc', KV_C), col('vc', KV_C)],
                           axis=2).astype(BF16)
    gl = jnp.swapaxes(col('gl', N_GATES).reshape(depth, d, C_HEADS, 3), 2, 3).reshape(depth, d, N_GATES)
    gl = jnp.concatenate([gl, jnp.zeros((depth, d, NGT - N_GATES), w_in.dtype)], axis=2)
    cols_t = jnp.concatenate([col('qa', A_W) * (A_QK ** -0.5 * LOG2E),
                              col('qb', B_W) * (HEAD_DIM ** -0.5 * LOG2E),
                              col('qc', C_W) * (HEAD_DIM ** -0.5 * LOG2E),
                              col('va', A_W), col('vb', KV_B), col('vs', KV_C), col('vw', KV_C), gl], axis=2)
    w_t = jnp.swapaxes(cols_t.astype(BF16), 1, 2)
    return w_rm, w_t


EXT_ROWS = 2 * BF16_ROWS
POS_SPLIT = 256


def _bf16_pieces(x, n=3):
    out, rest = [], np.float32(x)
    for _ in range(n):
        bits = np.array([rest], np.float32).view(np.uint32)
        bits = (bits + 0x7FFF + ((bits >> 16) & 1)) & np.uint32(0xFFFF0000)
        piece = bits.view(np.float32)[0]
        out.append(piece)
        rest = np.float32(rest - piece)
    return out


def _key_ext(s):
    pos = np.arange(s)
    ext = np.zeros((s, 2, LANES), np.float32)
    for j in range(3):
        ext[:, :, j] = (pos // POS_SPLIT)[:, None]
        ext[:, :, 3 + j] = (pos % POS_SPLIT)[:, None]
    blk = (pos % TK) // SLC_LEN
    ext[pos, 1, BF16_ROWS + blk] = 1.0
    return jnp.asarray(ext.reshape(s, 2 * LANES), dtype=BF16)


def _query_ext(slopes):
    ext = np.zeros((len(slopes) // 2, EXT_ROWS, 2, TQ), np.float32)
    for c, slope in enumerate(slopes):
        for j, piece in enumerate(_bf16_pieces(slope)):
            ext[c // 2, j, c % 2, :] = piece * POS_SPLIT
            ext[c // 2, 3 + j, c % 2, :] = piece
    return jnp.asarray(ext.reshape(len(slopes) // 2, EXT_ROWS, 2 * TQ), dtype=BF16)


def _in_proj_kernel(*refs, fused):
    if fused:
        y_ref, wo_ref, x_ref, nw_ref, w_ref, wt_ref, xo_ref, hk_ref, hg_ref, ht_ref, gt_ref = refs
        x = x_ref[...] + _dot(y_ref[...], wo_ref[...])
        xo_ref[...] = x
    else:
        x_ref, nw_ref, w_ref, wt_ref, hk_ref, hg_ref, ht_ref, gt_ref = refs
        x = x_ref[...]
    ms = jnp.mean(x * x, axis=-1, keepdims=True)
    xn = ((x * lax.rsqrt(ms + RMS_EPS)) * nw_ref[...]).astype(BF16)
    step = 256
    for c0 in range(0, NK, step):
        c1 = min(c0 + step, NK)
        hk_ref[:, c0:c1] = _dot(xn, w_ref[:, c0:c1]).astype(BF16)
    for c0 in range(0, NG, step):
        c1 = min(c0 + step, NG)
        g = _dot(xn, w_ref[:, NK + c0:NK + c1])
        hg_ref[:, c0:c1] = _silu(g) if c1 <= G_KC else g
    for r0 in range(0, NT, step):
        r1 = min(r0 + step, NT)
        ht_ref[r0:r1, :] = _dot_nt(wt_ref[r0:r1, :], xn).astype(BF16)
    gt_ref[...] = 1.0 / (1.0 + jnp.exp(-_dot_nt(wt_ref[NT:NT + NGT, :], xn)))


def _layer(a, l):
    return pl.BlockSpec((None,) + a.shape[1:], lambda *_: (l,) + (0,) * (a.ndim - 1))


def _in_proj(l, x2, nw, w_rm, w_t, y=None, w_out=None):
    n, d = x2.shape
    fused = y is not None
    row = lambda cols: pl.BlockSpec((TM, cols), lambda i: (i, 0))
    in_specs = [row(d), _layer(nw, l), _layer(w_rm, l), _layer(w_t, l)]
    out_specs = [row(NK), row(NG), pl.BlockSpec((NT, TM), lambda i: (0, i)), pl.BlockSpec((NGT, TM), lambda i: (0, i))]
    out_shape = [jax.ShapeDtypeStruct((n, NK), BF16), jax.ShapeDtypeStruct((n, NG), F32),
                 jax.ShapeDtypeStruct((NT, n), BF16), jax.ShapeDtypeStruct((NGT, n), F32)]
    args = (x2, nw, w_rm, w_t)
    if fused:
        in_specs = [row(D_MIX), _layer(w_out, l - 1)] + in_specs
        out_specs = [row(d)] + out_specs
        out_shape = [jax.ShapeDtypeStruct((n, d), F32)] + out_shape
        args = (y, w_out) + args
    return pl.pallas_call(
        functools.partial(_in_proj_kernel, fused=fused),
        grid=(n // TM,),
        in_specs=in_specs, out_specs=out_specs, out_shape=out_shape,
        compiler_params=_params("arbitrary"),
        name="out_in_proj" if fused else "in_proj",
    )(*args)


def _out_proj_kernel(y_ref, w_ref, x_ref, fw_ref, o_ref):
    acc = x_ref[...] + _dot(y_ref[...], w_ref[...])
    ms = jnp.mean(acc * acc, axis=-1, keepdims=True)
    o_ref[...] = (acc * lax.rsqrt(ms + RMS_EPS)) * fw_ref[...]


def _out_proj(l, y, w_out, x2, fw):
    n, d = x2.shape
    return pl.pallas_call(
        _out_proj_kernel,
        grid=(n // TM,),
        in_specs=[pl.BlockSpec((TM, D_MIX), lambda i: (i, 0)),
                  _layer(w_out, l),
                  pl.BlockSpec((TM, d), lambda i: (i, 0)),
                  pl.BlockSpec((1, d), lambda i: (0, 0))],
        out_specs=pl.BlockSpec((TM, d), lambda i: (i, 0)),
        out_shape=jax.ShapeDtypeStruct((n, d), F32),
        compiler_params=_params("arbitrary"),
        name="out_proj",
    )(y, w_out, x2, fw)


def _compress_kernel(kc_ref, vc_ref, pek_ref, pev_ref, wk1_ref, wk2_ref, wv1_ref, wv2_ref,
                     kcmp_ref, vcmpt_ref):
    nch = kc_ref.shape[0] // CMP_STRIDE
    left = lax.broadcasted_iota(jnp.int32, (nch, LANES), 1) < HEAD_DIM

    def one(x_ref, pe_ref, w1_ref, w2_ref, out_ref, transposed):
        u = [jnp.zeros((nch, CMP_HID), F32) for _ in range(C_KV)]
        v = [jnp.zeros((nch, CMP_HID), F32) for _ in range(C_KV)]
        for l in range(CMP_STRIDE):
            xl = x_ref[pl.ds(l, nch, stride=CMP_STRIDE), :]
            pe_u = pe_ref[l:l + 1, :]
            pe_v = pe_ref[CMP_STRIDE + l:CMP_STRIDE + l + 1, :]
            w_u = w1_ref[l]
            w_v = w1_ref[CMP_STRIDE + l]
            for g in range(C_KV):
                keep = left if g == 0 else jnp.logical_not(left)
                u[g] = u[g] + _dot(jnp.where(keep, xl + pe_u, 0.0).astype(BF16), w_u)
                v[g] = v[g] + _dot(jnp.where(keep, xl + pe_v, 0.0).astype(BF16), w_v)
        acc = jnp.zeros((LANES, nch) if transposed else (nch, LANES), F32)
        for g in range(C_KV):
            hid = u[g] + pltpu.roll(v[g], nch - 1, 0)
            act = _silu(hid).astype(BF16)
            acc = acc + (_dot_nt(w2_ref[g], act) if transposed else _dot(act, w2_ref[g]))
        out_ref[...] = acc.astype(BF16)

    one(kc_ref, pek_ref, wk1_ref, wk2_ref, kcmp_ref, False)
    one(vc_ref, pev_ref, wv1_ref, wv2_ref, vcmpt_ref, True)


def _compress(l, hg3, pek, pev, wk1, wk2, wv1, wv2t):
    b, s, _ = hg3.shape
    nch = s // CMP_STRIDE
    full = lambda a: _layer(a, l)
    return pl.pallas_call(
        _compress_kernel,
        grid=(b,),
        in_specs=[pl.BlockSpec((None, s, LANES), lambda i: (i, 0, G_KC // LANES)),
                  pl.BlockSpec((None, s, LANES), lambda i: (i, 0, G_VC // LANES)),
                  full(pek), full(pev), full(wk1), full(wk2), full(wv1), full(wv2t)],
        out_specs=[pl.BlockSpec((None, nch, LANES), lambda i: (i, 0, 0)),
                   pl.BlockSpec((None, LANES, nch), lambda i: (i, 0, 0))],
        out_shape=[jax.ShapeDtypeStruct((b, nch, LANES), BF16), jax.ShapeDtypeStruct((b, LANES, nch), BF16)],
        compiler_params=_params("arbitrary"),
        name="compress",
    )(hg3, hg3, pek, pev, wk1, wk2, wv1, wv2t)


def _fill_value_rows(vaug_ref, vt_ref, n_heads):
    s = vt_ref.shape[1]
    for h in range(n_heads):
        vaug_ref[h, 0:HEAD_DIM, :] = vt_ref[h * HEAD_DIM:(h + 1) * HEAD_DIM, :]
        vaug_ref[h, HEAD_DIM:V_ROWS, :] = jnp.ones((BF16_ROWS, s), BF16)


def _stack_gqa_q(qt_ref, qst_ref, heads_per_group):
    zero = jnp.zeros((HEAD_DIM, TQ), BF16)
    for p in range(qst_ref.shape[0]):
        cols = []
        for h in (2 * p, 2 * p + 1):
            q = qt_ref[h * HEAD_DIM:(h + 1) * HEAD_DIM, :]
            cols.append(jnp.concatenate([q, zero] if h // heads_per_group == 0 else [zero, q], axis=0))
        qst_ref[p, 0:LANES, :] = jnp.concatenate(cols, axis=1)


def _set_ext_rows(qst_ref, qx_ref):
    n = qst_ref.shape[0]
    qst_ref[:, LANES:LANES + EXT_ROWS, :] = qx_ref[...]
    qst_ref[:, LANES + EXT_ROWS:, :] = jnp.zeros((n, LANES - EXT_ROWS, 2 * TQ), BF16)


def _ext_keys(k_chunk, kx_ref, s0, selected):
    c0 = LANES if selected else 0
    return jnp.concatenate([k_chunk, kx_ref[pl.ds(s0, k_chunk.shape[0]), c0:c0 + LANES]], axis=1)


def _flash_pair(st, c0, madd, m_ref, e_ref):
    rows = st.shape[0]
    alphas = []
    for j in range(2):
        c = c0 + j
        u = st[:, j * TQ:(j + 1) * TQ]
        if madd is not None:
            u = u + madd
        m_old = m_ref[c:c + 1, :]
        m_new = jnp.maximum(m_old, jnp.max(u, axis=0, keepdims=True))
        e_ref[0:rows, c * TQ:(c + 1) * TQ] = jnp.exp2(u - m_new).astype(BF16)
        alphas.append(jnp.exp2(m_old - m_new))
        m_ref[c:c + 1, :] = m_new
    return alphas


def _accumulate(acc_ref, idx, vaug_chunk, e_ref, c0, alphas):
    a = jnp.concatenate(alphas, axis=1)
    rows = vaug_chunk.shape[1]
    acc_ref[idx] = acc_ref[idx] * a + _dot(vaug_chunk, e_ref[0:rows, c0 * TQ:(c0 + 2) * TQ])


def _pipelined(items, st_ref=None, lookahead=()):
    if st_ref is None:
        queue = [item[0]() for item in items[:AHEAD]]
    else:
        assert len(items) >= AHEAD and len(lookahead) in (0, AHEAD)
        queue = [functools.partial(lambda j: st_ref[j], j) for j in range(AHEAD)]
    todo = [item[0] for item in items[AHEAD:]] + [item[0] for item in lookahead]
    pending = None
    for _, softmax, values in items:
        if todo:
            queue.append(todo.pop(0)())
        if pending is not None:
            pending[0](pending[1])
        st = queue.pop(0)
        pending = (values, softmax(st() if callable(st) else st))
    pending[0](pending[1])
    for j, st in enumerate(queue):
        st_ref[j] = st


def _normalized(acc, lane0):
    return acc[0:HEAD_DIM, lane0:lane0 + TQ] / acc[HEAD_DIM:HEAD_DIM + 1, lane0:lane0 + TQ]


def _overlap_band(ncmp_pad, n_slc):
    ratio = SLC_LEN // CMP_STRIDE
    band = {}
    for j in range(n_slc):
        for c in range(ncmp_pad - 1):
            ov = min(c * CMP_STRIDE + CMP_LEN, (j + 1) * SLC_LEN) - max(c * CMP_STRIDE, j * SLC_LEN)
            if ov > 0:
                k = c - ratio * j
                assert band.setdefault(k, ov / CMP_LEN) == ov / CMP_LEN and -ratio <= k
    assert ratio * (n_slc - 1) + max(band) < ncmp_pad
    return ratio, band


def _mixers_kernel(sink_ref,
                   qa_ref, qb_ref, qc_ref, ka_ref, kb_ref, ks_ref, kw_ref, va_ref, vb_ref, vs_ref, vw_ref,
                   kcmp_ref, vcmpt_ref, ga_ref, gb_ref, gc_ref, glt_ref,
                   lq1_ref, lk1_ref, lq2_ref, lk2_ref, sub_ref, kx_ref, qxa_ref, qxb_ref, qxc_ref,
                   y_ref,
                   qsta_ref, qstb_ref, qstc_ref, vauga_ref, vaugb_ref, vaugc_ref,
                   ea_ref, eb_ref, ec_ref, acca_ref, accb_ref, accc_ref, ma_ref, mb_ref, mc_ref,
                   msel_ref, st_ref, psum_ref, *, layer, lam_init):
    qi = pl.program_id(1)
    s_len = ka_ref.shape[0]
    n_slc = s_len // SLC_LEN
    ncmp = kcmp_ref.shape[0]
    blocks_per_chunk = TK // SLC_LEN
    cpg = C_HEADS // C_KV

    @pl.when(qi == 0)
    def _():
        _fill_value_rows(vauga_ref, va_ref, A_HEADS)
        _fill_value_rows(vaugb_ref, vb_ref, B_KV)
        _fill_value_rows(vaugc_ref.at[0], vs_ref, C_KV)
        _fill_value_rows(vaugc_ref.at[1], vw_ref, C_KV)
        _set_ext_rows(qsta_ref, qxa_ref)
        _set_ext_rows(qstb_ref, qxb_ref)
        _set_ext_rows(qstc_ref, qxc_ref)

    sub32 = lax.broadcasted_iota(jnp.int32, (LANES, TQ), 0) // A_QK
    for h in range(A_HEADS):
        blk = h * HEAD_DIM // LANES
        q = qa_ref[blk * LANES:(blk + 1) * LANES, :]
        zero = jnp.zeros_like(q)
        first = 2 * (h % (LANES // HEAD_DIM))
        qsta_ref[h, 0:LANES, :] = jnp.concatenate(
            [jnp.where(sub32 == first, q, zero), jnp.where(sub32 == first + 1, q, zero)], axis=1)
    _stack_gqa_q(qb_ref, qstb_ref, B_HEADS // B_KV)
    _stack_gqa_q(qc_ref, qstc_ref, cpg)
    acca_ref[...] = jnp.zeros(acca_ref.shape, F32)
    accc_ref[...] = jnp.zeros(accc_ref.shape, F32)
    t0 = qi * TQ
    den_rows = lax.broadcasted_iota(jnp.int32, (V_ROWS, 2 * TQ), 0) >= HEAD_DIM
    for g in range(B_KV):
        accb_ref[g] = jnp.where(den_rows, 1.0, 0.0)
    t_abs = (t0 + lax.broadcasted_iota(jnp.int32, (1, TQ), 1)).astype(F32)
    for c in range(B_HEADS):
        mb_ref[c:c + 1, :] = sink_ref[layer, c] * LOG2E + _SLOPE_B[c] * t_abs
    ma_ref[...] = jnp.full(ma_ref.shape, NEG, F32)
    mc_ref[...] = jnp.full(mc_ref.shape, NEG, F32)
    row_i = lax.broadcasted_iota(jnp.int32, (TK, TQ), 0)
    row_f = row_i.astype(F32)
    kmq = (row_i - lax.broadcasted_iota(jnp.int32, (TK, TQ), 1)).astype(F32)

    def start(ch):
        return pl.multiple_of(ch * TK, TK)

    masks = {}

    def cached(tag, build):
        if tag not in masks:
            masks[tag] = build()
        return masks[tag]

    def causal_mask(tag, s0):
        return cached(tag, lambda: jnp.where(kmq + (s0 - t0).astype(F32) <= 0.0, 0.0, NEG))

    def make_item(keys, weights, rows, madd, m_ref, e_ref, c0, acc_ref, idx, values):
        def scores():
            return _dot(keys(), weights())

        def softmax(st):
            return _flash_pair(st[0:rows], c0, madd(), m_ref, e_ref)

        def accumulate(alphas):
            _accumulate(acc_ref, idx, values(), e_ref, c0, alphas)

        return scores, softmax, accumulate

    def item_a(tag, ch, diagonal, h):
        blk = h * HEAD_DIM // LANES
        return make_item(
            lambda: _ext_keys(ka_ref[pl.ds(start(ch), TK), blk * LANES:(blk + 1) * LANES], kx_ref, start(ch), False),
            lambda: qsta_ref[h], TK,
            lambda: causal_mask(tag, start(ch)) if diagonal else None,
            ma_ref, ea_ref, 2 * h, acca_ref, h, lambda: vauga_ref[h, :, pl.ds(start(ch), TK)])

    def item_s(tag, ch, diagonal, p):
        g = 2 * p // cpg

        def weights():
            rows = msel_ref[g, ch]
            qstc_ref[p, LANES + BF16_ROWS:LANES + EXT_ROWS, :] = jnp.concatenate([rows, rows], axis=1)
            return qstc_ref[p]

        return make_item(
            lambda: _ext_keys(ks_ref[pl.ds(start(ch), TK), :], kx_ref, start(ch), True), weights, TK,
            lambda: causal_mask(tag, start(ch)) if diagonal else None,
            mc_ref.at[0], ec_ref, 2 * p, accc_ref.at[0], p, lambda: vaugc_ref[0, g, :, pl.ds(start(ch), TK)])

    def item_w(r, p):
        g = 2 * p // cpg
        n_full = C_WINDOW // TK
        rows = TK if r < n_full else TQ
        nominal = t0 - C_WINDOW + r * TK
        s0 = pl.multiple_of(jnp.maximum(nominal, 0), TQ)

        def madd():
            if r == n_full:
                return cached('own', lambda: jnp.where(kmq[0:rows] <= 0.0, 0.0, NEG))

            def build():
                ok = row_f < (nominal + TK - s0).astype(F32)
                if r == 0:
                    ok = jnp.logical_and(ok, kmq + (s0 - t0).astype(F32) > -float(C_WINDOW))
                return jnp.where(ok, 0.0, NEG)
            return cached(('win', r), build)

        return make_item(
            lambda: _ext_keys(kw_ref[pl.ds(s0, rows), :], kx_ref, s0, False), lambda: qstc_ref[p], rows, madd,
            mc_ref.at[1], ec_ref, 2 * p, accc_ref.at[1], p, lambda: vaugc_ref[1, g, :, pl.ds(s0, rows)])

    def item_b(g):
        s0 = pl.multiple_of(jnp.maximum(t0 - B_WINDOW, 0), LANES)

        def madd():
            nd = kmq + (s0 - t0).astype(F32)
            return jnp.where(jnp.logical_and(nd <= 0.0, nd > -float(B_WINDOW)), 0.0, NEG)

        return make_item(
            lambda: _ext_keys(kb_ref[pl.ds(s0, TK), :], kx_ref, s0, False), lambda: qstb_ref[g], TK, madd,
            mb_ref, eb_ref, 2 * g, accb_ref, g, lambda: vaugb_ref[g, :, pl.ds(s0, TK)])

    def causal_items(tag, ch, diagonal):
        return ([item_a(tag, ch, diagonal, h) for h in range(A_HEADS)]
                + [item_s(tag, ch, diagonal, p) for p in range(C_HEADS // 2)])

    last = (t0 + TQ - 1) // TK
    first = causal_items('first', 0, False)[:AHEAD]
    for j, item in enumerate(first[:A_HEADS]):
        st_ref[j] = item[0]()

    p_cmp = []
    tpos = t0 + lax.broadcasted_iota(jnp.int32, (ncmp, TQ), 1)
    cend = lax.broadcasted_iota(jnp.int32, (ncmp, TQ), 0) * CMP_STRIDE + (CMP_LEN - 1)
    dist_c = (tpos - cend).astype(F32)
    valid_c = dist_c >= 0.0
    for p in range(C_HEADS // 2):
        st = _dot(kcmp_ref[...], qstc_ref[p, 0:LANES, :])
        for j in range(2):
            c = 2 * p + j
            sc = jnp.where(valid_c, st[:, j * TQ:(j + 1) * TQ] - _SLOPE_C[c] * dist_c, NEG)
            mx = jnp.max(sc, axis=0, keepdims=True)
            e = jnp.where(valid_c, jnp.exp2(sc - mx), 0.0)
            den = jnp.sum(e, axis=0, keepdims=True)
            p_cmp.append(e / jnp.where(den > 0.0, den, 1.0))
    o_cmp = [_dot(vcmpt_ref[g * HEAD_DIM:(g + 1) * HEAD_DIM, :],
                  jnp.concatenate(p_cmp[g * cpg:(g + 1) * cpg], axis=1).astype(BF16))
             for g in range(C_KV)]

    jidx = lax.broadcasted_iota(jnp.int32, (n_slc, TQ), 0)
    tl = t0 + lax.broadcasted_iota(jnp.int32, (n_slc, TQ), 1)
    cur = tl // SLC_LEN
    causal_b = jidx * SLC_LEN <= tl
    forced = jnp.logical_or(jidx == 0, jnp.logical_or(jidx == cur, jidx == cur - 1))
    ratio, band = _overlap_band(ncmp, n_slc)
    for g in range(C_KV):
        psum = p_cmp[g * cpg]
        for i in range(1, cpg):
            psum = psum + p_cmp[g * cpg + i]
        psum_ref[g] = psum
        imp = jnp.zeros((n_slc, TQ), F32)
        for k, coef in band.items():
            if k >= 0:
                rows = psum_ref[g, pl.ds(k, n_slc, stride=ratio), :]
            else:
                rows = pltpu.roll(psum_ref[g, pl.ds(ratio + k, n_slc, stride=ratio), :], 1, 0)
                rows = jnp.where(jidx == 0, 0.0, rows)
            imp = imp + coef * rows
        score = jnp.where(causal_b, jnp.where(forced, FORCE, imp), NEG)
        rank = jnp.zeros((n_slc, TQ), F32)
        for jp in range(n_slc):
            row = score[jp:jp + 1, :]
            tie = jnp.where(jidx > jp, 1.0, 0.0)
            rank = rank + jnp.where(row > score, 1.0, jnp.where(row == score, tie, 0.0))
        msel = jnp.where(rank < float(SLC_TOPN), 0.0, NEG)
        pad = jnp.zeros((BF16_ROWS - blocks_per_chunk, TQ), F32)
        for ch in range(n_slc // blocks_per_chunk):
            rows = msel[ch * blocks_per_chunk:(ch + 1) * blocks_per_chunk, :]
            msel_ref[g, ch] = jnp.concatenate([rows, pad], axis=0).astype(BF16)

    def two_chunks(i, carry):
        masks.clear()
        _pipelined(causal_items(0, 2 * i, False) + causal_items(1, 2 * i + 1, False), st_ref,
                   causal_items(2, 2 * i + 2, False)[:AHEAD])
        masks.clear()
        return carry

    lax.fori_loop(0, last // 2, two_chunks, 0)

    def tail_items():
        return ([item_w(r, p) for r in range(C_WINDOW // TK + 1) for p in range(C_HEADS // 2)]
                + [item_b(g) for g in range(B_KV)])

    @pl.when(last % 2 == 1)
    def _():
        masks.clear()
        _pipelined(causal_items(0, last - 1, False) + causal_items(1, last, True) + tail_items(), st_ref)
        masks.clear()

    @pl.when(last % 2 == 0)
    def _():
        masks.clear()
        _pipelined(causal_items(1, last, True) + tail_items(), st_ref)
        masks.clear()

    def emit(col0, outs, g_ref):
        for i in range(len(outs) // 2):
            o = jnp.transpose(jnp.concatenate(outs[2 * i:2 * i + 2], axis=0))
            gate = g_ref[:, i * LANES:(i + 1) * LANES]
            y_ref[:, col0 + i * LANES:col0 + (i + 1) * LANES] = (o * gate).astype(BF16)

    lam = (jnp.exp(jnp.sum(lq1_ref[...] * lk1_ref[...], axis=1, keepdims=True))
           - jnp.exp(jnp.sum(lq2_ref[...] * lk2_ref[...], axis=1, keepdims=True)) + lam_init)
    outs = []
    for h in range(A_HEADS):
        acc = acca_ref[h]
        d = _normalized(acc, 0) - lam * _normalized(acc, TQ)
        ms = jnp.mean(d * d, axis=0, keepdims=True)
        outs.append(((d * lax.rsqrt(ms + RMS_EPS)) * sub_ref[...]) * (1.0 - lam_init))
    emit(0, outs, ga_ref)

    outs = []
    for c in range(B_HEADS):
        outs.append(_normalized(accb_ref[c // 2], (c % 2) * TQ))
    emit(A_W, outs, gb_ref)

    sig = glt_ref[...]
    outs = []
    for c in range(C_HEADS):
        g, i = c // cpg, c % cpg
        gate = lambda br: sig[br * C_HEADS + c:br * C_HEADS + c + 1, :]
        lane0 = (c % 2) * TQ
        outs.append(gate(0) * o_cmp[g][:, i * TQ:(i + 1) * TQ]
                    + gate(1) * _normalized(accc_ref[0, c // 2], lane0)
                    + gate(2) * _normalized(accc_ref[1, c // 2], lane0))
    emit(A_W + B_W, outs, gc_ref)


def _mixers(l, sinks, hk3, hg3, ht, gt, kcmp, vcmpt, lq1, lk1, lq2, lk2, sub, lam_init):
    b, s, _ = hk3.shape
    assert B_WINDOW + TQ <= TK and B_HEADS <= SUBLANES and C_WINDOW % TK == 0 and TQ % BF16_ROWS == 0
    assert AHEAD <= A_HEADS
    n_slc = s // SLC_LEN
    ncmp = kcmp.shape[1]
    nq = s // TQ
    kx = _key_ext(s)
    qxa = _query_ext([_SLOPE_A[c // 2] for c in range(2 * A_HEADS)])
    qxb = _query_ext(_SLOPE_B)
    qxc = _query_ext(_SLOPE_C)
    const = lambda a: pl.BlockSpec(a.shape, lambda bi, qi: (0,) * a.ndim)
    qt = lambda off, rows: pl.BlockSpec((rows, TQ), lambda bi, qi: (off // rows, bi * nq + qi))
    vt = lambda off, rows: pl.BlockSpec((rows, s), lambda bi, qi: (off // rows, bi))
    krow = lambda off, cols: pl.BlockSpec((None, s, cols), lambda bi, qi: (bi, 0, off // cols))
    gate = lambda off, cols: pl.BlockSpec((None, TQ, cols), lambda bi, qi: (bi, qi, off // cols))
    per_batch = lambda a: pl.BlockSpec((None,) + a.shape[1:], lambda bi, qi: (bi, 0, 0))
    return pl.pallas_call(
        functools.partial(_mixers_kernel, layer=l, lam_init=lam_init),
        grid=(b, nq),
        in_specs=[pl.BlockSpec(memory_space=pltpu.SMEM),
                  qt(T_QA, A_W), qt(T_QB, B_W), qt(T_QC, C_W),
                  krow(K_KA, A_W), krow(K_KB, KV_B), krow(K_KS, KV_C), krow(K_KW, KV_C),
                  vt(T_VA, A_W), vt(T_VB, KV_B), vt(T_VS, KV_C), vt(T_VW, KV_C),
                  per_batch(kcmp), per_batch(vcmpt),
                  gate(G_GA, A_W), gate(G_GB, B_W), gate(G_GC, C_W),
                  pl.BlockSpec((NGT, TQ), lambda bi, qi: (0, bi * nq + qi)),
                  _layer(lq1, l), _layer(lk1, l), _layer(lq2, l), _layer(lk2, l), _layer(sub, l),
                  const(kx), const(qxa), const(qxb), const(qxc)],
        out_specs=pl.BlockSpec((None, TQ, D_MIX), lambda bi, qi: (bi, qi, 0)),
        out_shape=jax.ShapeDtypeStruct((b, s, D_MIX), BF16),
        scratch_shapes=[pltpu.VMEM((A_HEADS, 2 * LANES, 2 * TQ), BF16),
                        pltpu.VMEM((B_HEADS // 2, 2 * LANES, 2 * TQ), BF16),
                        pltpu.VMEM((C_HEADS // 2, 2 * LANES, 2 * TQ), BF16),
                        pltpu.VMEM((A_HEADS, V_ROWS, s), BF16),
                        pltpu.VMEM((B_KV, V_ROWS, s), BF16),
                        pltpu.VMEM((2, C_KV, V_ROWS, s), BF16),
                        pltpu.VMEM((TK, 2 * A_HEADS * TQ), BF16),
                        pltpu.VMEM((TK, B_HEADS * TQ), BF16),
                        pltpu.VMEM((TK, C_HEADS * TQ), BF16),
                        pltpu.VMEM((A_HEADS, V_ROWS, 2 * TQ), F32),
                        pltpu.VMEM((B_HEADS // 2, V_ROWS, 2 * TQ), F32),
                        pltpu.VMEM((2, C_HEADS // 2, V_ROWS, 2 * TQ), F32),
                        pltpu.VMEM((2 * A_HEADS, TQ), F32),
                        pltpu.VMEM((SUBLANES, TQ), F32),
                        pltpu.VMEM((2, C_HEADS, TQ), F32),
                        pltpu.VMEM((C_KV, s // TK, BF16_ROWS, TQ), BF16),
                        pltpu.VMEM((AHEAD, TK, 2 * TQ), F32),
                        pltpu.VMEM((C_KV, ncmp, TQ), F32)],
        compiler_params=_params("arbitrary", "arbitrary"),
        name="mixers",
    )(sinks, ht, ht, ht, hk3, hk3, hk3, hk3, ht, ht, ht, ht, kcmp, vcmpt, hg3, hg3, hg3, gt,
      lq1, lk1, lq2, lk2, sub, kx, qxa, qxb, qxc)


def kernel(x, norm_w, w_in, w_out, diff_lq1, diff_lk1, diff_lq2, diff_lk2, diff_subln, sinks,
           cmp_pe_k, cmp_pe_v, cmp_wk1, cmp_wk2, cmp_wv1, cmp_wv2, final_norm):
    b, s, d = x.shape
    depth = w_in.shape[0]
    assert s % TK == 0 and s // TK >= 3 and (b * s) % TM == 0 and d % LANES == 0

    w_rm, w_t = _prep_w_in(w_in)
    w_out_b = w_out.astype(BF16)
    pe2 = lambda pe: jnp.concatenate([pe, pe], axis=-1)
    w1 = lambda w: jnp.concatenate([w.reshape(depth, CMP_LEN, HEAD_DIM, CMP_HID)] * 2, axis=2).astype(BF16)
    zero2 = jnp.zeros((depth, CMP_HID, HEAD_DIM), F32)
    w2 = lambda w: jnp.stack([jnp.concatenate([w, zero2], axis=-1),
                              jnp.concatenate([zero2, w], axis=-1)], axis=1)
    pek, pev = pe2(cmp_pe_k), pe2(cmp_pe_v)
    wk1, wv1 = w1(cmp_wk1), w1(cmp_wv1)
    wk2 = w2(cmp_wk2).astype(BF16)
    wv2t = jnp.swapaxes(w2(cmp_wv2), 2, 3).astype(BF16)
    sub = jnp.broadcast_to(diff_subln[:, :, None], (depth, HEAD_DIM, TQ))
    fw = final_norm[None, :]
    nw = norm_w[:, None, :]
    lam_rows = [a[:, None, :] for a in (diff_lq1, diff_lk1, diff_lq2, diff_lk2)]
    x2 = x.reshape(b * s, d)
    y = None
    for l in range(depth):
        if y is None:
            hk, hg, ht, gt = _in_proj(l, x2, nw, w_rm, w_t)
        else:
            x2, hk, hg, ht, gt = _in_proj(l, x2, nw, w_rm, w_t, y, w_out_b)
        hk3 = hk.reshape(b, s, NK)
        hg3 = hg.reshape(b, s, NG)
        kcmp, vcmpt = _compress(l, hg3, pek, pev, wk1, wk2, wv1, wv2t)
        lam_init = 0.8 - 0.6 * math.exp(-0.3 * l)
        y = _mixers(l, sinks, hk3, hg3, ht, gt, kcmp, vcmpt, *lam_rows, sub, lam_init).reshape(b * s, D_MIX)
    return _out_proj(depth - 1, y, w_out_b, x2, fw).reshape(b, s, d)
```

```python
import functools
import math

import numpy as np
import jax
import jax.numpy as jnp
from jax import lax
from jax.experimental import pallas as pl
from jax.experimental.pallas import tpu as pltpu

F32 = jnp.float32
BF16 = jnp.bfloat16

HEAD_DIM = 64
RMS_EPS = 1e-6
NEG = -1e30
FORCE = 1e30

A_HEADS = 4
A_QK = HEAD_DIM // 2
B_HEADS = 4
B_KV = 2
B_WINDOW = 128
C_HEADS = 8
C_KV = 2
CMP_LEN = 32
CMP_STRIDE = 16
CMP_HID = 128
SLC_LEN = 64
SLC_TOPN = 8
C_WINDOW = 512

A_W = A_HEADS * HEAD_DIM
B_W = B_HEADS * HEAD_DIM
C_W = C_HEADS * HEAD_DIM
D_MIX = A_W + B_W + C_W
KV_B = B_KV * HEAD_DIM
KV_C = C_KV * HEAD_DIM
N_GATES = 3 * C_HEADS

LANES = 128
SUBLANES = 8
BF16_ROWS = 16
TQ = 128
TK = 256
TM = 512
TM_PROJ = 1024
VMEM_LIMIT_PROJ = 56 * 1024 * 1024
AHEAD = 4
V_ROWS = HEAD_DIM + BF16_ROWS
VMEM_LIMIT = 48 * 1024 * 1024

LOG2E = math.log2(math.e)

_O = dict(qa=0, ka=256, va=512, ga=768, qb=1024, kb=1280, vb=1408, gb=1536, qc=1792,
          kc=2304, vc=2432, ks=2560, vs=2688, kw=2816, vw=2944, gl=3072, gc=3096)

K_KA, K_KB, K_KS, K_KW = 0, 256, 384, 512
NK = 640
G_GA, G_GB, G_GC, G_KC, G_VC = 0, 256, 512, 1024, 1152
NG = 1280
T_QA, T_QB, T_QC, T_VA, T_VB, T_VS, T_VW = 0, 256, 512, 1024, 1280, 1408, 1536
NT = 1664
NGT = 32


def _slopes(n):
    return [float(2.0 ** (-8.0 * h / n)) * LOG2E for h in range(1, n + 1)]


_SLOPE_A = _slopes(A_HEADS)
_SLOPE_B = _slopes(B_HEADS)
_SLOPE_C = _slopes(C_HEADS)


def _params(*sem):
    return pltpu.CompilerParams(dimension_semantics=sem, vmem_limit_bytes=VMEM_LIMIT)


def _dot_nt(a, b):
    return lax.dot_general(a, b, (((1,), (1,)), ((), ())), preferred_element_type=F32)


def _dot(a, b):
    return jnp.dot(a, b, preferred_element_type=F32)


def _silu(g):
    return g * (1.0 / (1.0 + jnp.exp(-g)))


def _prep_w_in(w_in):
    depth, d, _ = w_in.shape
    col = lambda name, width: w_in[:, :, _O[name]:_O[name] + width]
    w_rm = jnp.concatenate([col('ka', A_W), col('kb', KV_B), col('ks', KV_C), col('kw', KV_C),
                            col('ga', A_W), col('gb', B_W), col('gc', C_W), col('kc', KV_C), col('vc', KV_C)],
                           axis=2).astype(BF16)
    gl = jnp.swapaxes(col('gl', N_GATES).reshape(depth, d, C_HEADS, 3), 2, 3).reshape(depth, d, N_GATES)
    gl = jnp.concatenate([gl, jnp.zeros((depth, d, NGT - N_GATES), w_in.dtype)], axis=2)
    cols_t = jnp.concatenate([col('qa', A_W) * (A_QK ** -0.5 * LOG2E),
                              col('qb', B_W) * (HEAD_DIM ** -0.5 * LOG2E),
                              col('qc', C_W) * (HEAD_DIM ** -0.5 * LOG2E),
                              col('va', A_W), col('vb', KV_B), col('vs', KV_C), col('vw', KV_C), gl], axis=2)
    w_t = jnp.swapaxes(cols_t.astype(BF16), 1, 2)
    return w_rm, w_t


EXT_ROWS = 2 * BF16_ROWS
POS_SPLIT = 256


def _bf16_pieces(x, n=3):
    out, rest = [], np.float32(x)
    for _ in range(n):
        bits = np.array([rest], np.float32).view(np.uint32)
        bits = (bits + 0x7FFF + ((bits >> 16) & 1)) & np.uint32(0xFFFF0000)
        piece = bits.view(np.float32)[0]
        out.append(piece)
        rest = np.float32(rest - piece)
    return out


def _key_ext(s):
    pos = np.arange(s)
    ext = np.zeros((s, 2, LANES), np.float32)
    for j in range(3):
        ext[:, :, j] = (pos // POS_SPLIT)[:, None]
        ext[:, :, 3 + j] = (pos % POS_SPLIT)[:, None]
    blk = (pos % TK) // SLC_LEN
    ext[pos, 1, BF16_ROWS + blk] = 1.0
    return jnp.asarray(ext.reshape(s, 2 * LANES), dtype=BF16)


def _query_ext(slopes):
    ext = np.zeros((len(slopes) // 2, EXT_ROWS, 2, TQ), np.float32)
    for c, slope in enumerate(slopes):
        for j, piece in enumerate(_bf16_pieces(slope)):
            ext[c // 2, j, c % 2, :] = piece * POS_SPLIT
            ext[c // 2, 3 + j, c % 2, :] = piece
    return jnp.asarray(ext.reshape(len(slopes) // 2, EXT_ROWS, 2 * TQ), dtype=BF16)


def _in_proj_kernel(*refs, fused):
    if fused:
        y_ref, wo_ref, x_ref, nw_ref, w_ref, wt_ref, xo_ref, hk_ref, hg_ref, ht_ref, gt_ref = refs
        x = x_ref[...] + _dot(y_ref[...], wo_ref[...])
        xo_ref[...] = x
    else:
        x_ref, nw_ref, w_ref, wt_ref, hk_ref, hg_ref, ht_ref, gt_ref = refs
        x = x_ref[...]
    ms = jnp.mean(x * x, axis=-1, keepdims=True)
    xn = ((x * lax.rsqrt(ms + RMS_EPS)) * nw_ref[...]).astype(BF16)
    step = 256
    for c0 in range(0, NK, step):
        c1 = min(c0 + step, NK)
        hk_ref[:, c0:c1] = _dot(xn, w_ref[:, c0:c1]).astype(BF16)
    for c0 in range(0, NG, step):
        c1 = min(c0 + step, NG)
        g = _dot(xn, w_ref[:, NK + c0:NK + c1])
        hg_ref[:, c0:c1] = _silu(g) if c1 <= G_KC else g
    for r0 in range(0, NT, step):
        r1 = min(r0 + step, NT)
        ht_ref[r0:r1, :] = _dot_nt(wt_ref[r0:r1, :], xn).astype(BF16)
    gt_ref[...] = 1.0 / (1.0 + jnp.exp(-_dot_nt(wt_ref[NT:NT + NGT, :], xn)))


def _layer(a, l, single=False):
    return pl.BlockSpec((None,) + a.shape[1:], lambda *_: (l,) + (0,) * (a.ndim - 1),
                        pipeline_mode=pl.Buffered(1) if single else None)


def _in_proj(l, x2, nw, w_rm, w_t, y=None, w_out=None):
    n, d = x2.shape
    fused = y is not None
    tm = TM_PROJ
    row = lambda cols: pl.BlockSpec((tm, cols), lambda i: (i, 0))
    in_specs = [row(d), _layer(nw, l), _layer(w_rm, l, single=True), _layer(w_t, l, single=True)]
    out_specs = [row(NK), row(NG), pl.BlockSpec((NT, tm), lambda i: (0, i)), pl.BlockSpec((NGT, tm), lambda i: (0, i))]
    out_shape = [jax.ShapeDtypeStruct((n, NK), BF16), jax.ShapeDtypeStruct((n, NG), F32),
                 jax.ShapeDtypeStruct((NT, n), BF16), jax.ShapeDtypeStruct((NGT, n), F32)]
    args = (x2, nw, w_rm, w_t)
    if fused:
        in_specs = [row(D_MIX), _layer(w_out, l - 1, single=True)] + in_specs
        out_specs = [row(d)] + out_specs
        out_shape = [jax.ShapeDtypeStruct((n, d), F32)] + out_shape
        args = (y, w_out) + args
    return pl.pallas_call(
        functools.partial(_in_proj_kernel, fused=fused),
        grid=(n // tm,),
        in_specs=in_specs, out_specs=out_specs, out_shape=out_shape,
        compiler_params=pltpu.CompilerParams(dimension_semantics=("arbitrary",), vmem_limit_bytes=VMEM_LIMIT_PROJ),
        name="out_in_proj" if fused else "in_proj",
    )(*args)


def _out_proj_kernel(y_ref, w_ref, x_ref, fw_ref, o_ref):
    acc = x_ref[...] + _dot(y_ref[...], w_ref[...])
    ms = jnp.mean(acc * acc, axis=-1, keepdims=True)
    o_ref[...] = (acc * lax.rsqrt(ms + RMS_EPS)) * fw_ref[...]


def _out_proj(l, y, w_out, x2, fw):
    n, d = x2.shape
    return pl.pallas_call(
        _out_proj_kernel,
        grid=(n // TM,),
        in_specs=[pl.BlockSpec((TM, D_MIX), lambda i: (i, 0)),
                  _layer(w_out, l),
                  pl.BlockSpec((TM, d), lambda i: (i, 0)),
                  pl.BlockSpec((1, d), lambda i: (0, 0))],
        out_specs=pl.BlockSpec((TM, d), lambda i: (i, 0)),
        out_shape=jax.ShapeDtypeStruct((n, d), F32),
        compiler_params=_params("arbitrary"),
        name="out_proj",
    )(y, w_out, x2, fw)


def _compress_kernel(kc_ref, vc_ref, pek_ref, pev_ref, wk1_ref, wk2_ref, wv1_ref, wv2_ref,
                     kcmp_ref, vcmpt_ref):
    nch = kc_ref.shape[0] // CMP_STRIDE
    left = lax.broadcasted_iota(jnp.int32, (nch, LANES), 1) < HEAD_DIM

    def one(x_ref, pe_ref, w1_ref, w2_ref, out_ref, transposed):
        u = [jnp.zeros((nch, CMP_HID), F32) for _ in range(C_KV)]
        v = [jnp.zeros((nch, CMP_HID), F32) for _ in range(C_KV)]
        for l in range(CMP_STRIDE):
            xl = x_ref[pl.ds(l, nch, stride=CMP_STRIDE), :]
            pe_u = pe_ref[l:l + 1, :]
            pe_v = pe_ref[CMP_STRIDE + l:CMP_STRIDE + l + 1, :]
            w_u = w1_ref[l]
            w_v = w1_ref[CMP_STRIDE + l]
            for g in range(C_KV):
                keep = left if g == 0 else jnp.logical_not(left)
                u[g] = u[g] + _dot(jnp.where(keep, xl + pe_u, 0.0).astype(BF16), w_u)
                v[g] = v[g] + _dot(jnp.where(keep, xl + pe_v, 0.0).astype(BF16), w_v)
        acc = jnp.zeros((LANES, nch) if transposed else (nch, LANES), F32)
        for g in range(C_KV):
            hid = u[g] + pltpu.roll(v[g], nch - 1, 0)
            act = _silu(hid).astype(BF16)
            acc = acc + (_dot_nt(w2_ref[g], act) if transposed else _dot(act, w2_ref[g]))
        out_ref[...] = acc.astype(BF16)

    one(kc_ref, pek_ref, wk1_ref, wk2_ref, kcmp_ref, False)
    one(vc_ref, pev_ref, wv1_ref, wv2_ref, vcmpt_ref, True)


def _compress(l, hg3, pek, pev, wk1, wk2, wv1, wv2t):
    b, s, _ = hg3.shape
    nch = s // CMP_STRIDE
    full = lambda a: _layer(a, l)
    return pl.pallas_call(
        _compress_kernel,
        grid=(b,),
        in_specs=[pl.BlockSpec((None, s, LANES), lambda i: (i, 0, G_KC // LANES)),
                  pl.BlockSpec((None, s, LANES), lambda i: (i, 0, G_VC // LANES)),
                  full(pek), full(pev), full(wk1), full(wk2), full(wv1), full(wv2t)],
        out_specs=[pl.BlockSpec((None, nch, LANES), lambda i: (i, 0, 0)),
                   pl.BlockSpec((None, LANES, nch), lambda i: (i, 0, 0))],
        out_shape=[jax.ShapeDtypeStruct((b, nch, LANES), BF16), jax.ShapeDtypeStruct((b, LANES, nch), BF16)],
        compiler_params=_params("arbitrary"),
        name="compress",
    )(hg3, hg3, pek, pev, wk1, wk2, wv1, wv2t)


def _fill_value_rows(vaug_ref, vt_ref, n_heads):
    s = vt_ref.shape[1]
    for h in range(n_heads):
        vaug_ref[h, 0:HEAD_DIM, :] = vt_ref[h * HEAD_DIM:(h + 1) * HEAD_DIM, :]
        vaug_ref[h, HEAD_DIM:V_ROWS, :] = jnp.ones((BF16_ROWS, s), BF16)


def _stack_gqa_q(qt_ref, qst_ref, heads_per_group):
    zero = jnp.zeros((HEAD_DIM, TQ), BF16)
    for p in range(qst_ref.shape[0]):
        cols = []
        for h in (2 * p, 2 * p + 1):
            q = qt_ref[h * HEAD_DIM:(h + 1) * HEAD_DIM, :]
            cols.append(jnp.concatenate([q, zero] if h // heads_per_group == 0 else [zero, q], axis=0))
        qst_ref[p, 0:LANES, :] = jnp.concatenate(cols, axis=1)


def _set_ext_rows(qst_ref, qx_ref):
    n = qst_ref.shape[0]
    qst_ref[:, LANES:LANES + EXT_ROWS, :] = qx_ref[...]
    qst_ref[:, LANES + EXT_ROWS:, :] = jnp.zeros((n, LANES - EXT_ROWS, 2 * TQ), BF16)


def _ext_keys(k_chunk, kx_ref, s0, selected):
    c0 = LANES if selected else 0
    return jnp.concatenate([k_chunk, kx_ref[pl.ds(s0, k_chunk.shape[0]), c0:c0 + LANES]], axis=1)


def _flash_pair(st, c0, madd, m_ref, e_ref):
    rows = st.shape[0]
    alphas = []
    for j in range(2):
        c = c0 + j
        u = st[:, j * TQ:(j + 1) * TQ]
        if madd is not None:
            u = u + madd
        m_old = m_ref[c:c + 1, :]
        m_new = jnp.maximum(m_old, jnp.max(u, axis=0, keepdims=True))
        e_ref[0:rows, c * TQ:(c + 1) * TQ] = jnp.exp2(u - m_new).astype(BF16)
        alphas.append(jnp.exp2(m_old - m_new))
        m_ref[c:c + 1, :] = m_new
    return alphas


def _accumulate(acc_ref, idx, vaug_chunk, e_ref, c0, alphas):
    a = jnp.concatenate(alphas, axis=1)
    rows = vaug_chunk.shape[1]
    acc_ref[idx] = acc_ref[idx] * a + _dot(vaug_chunk, e_ref[0:rows, c0 * TQ:(c0 + 2) * TQ])


def _pipelined(items, st_ref=None, lookahead=()):
    if st_ref is None:
        queue = [item[0]() for item in items[:AHEAD]]
    else:
        assert len(items) >= AHEAD and len(lookahead) in (0, AHEAD)
        queue = [functools.partial(lambda j: st_ref[j], j) for j in range(AHEAD)]
    todo = [item[0] for item in items[AHEAD:]] + [item[0] for item in lookahead]
    pending = None
    for _, softmax, values in items:
        if todo:
            queue.append(todo.pop(0)())
        if pending is not None:
            pending[0](pending[1])
        st = queue.pop(0)
        pending = (values, softmax(st() if callable(st) else st))
    pending[0](pending[1])
    for j, st in enumerate(queue):
        st_ref[j] = st


def _normalized(acc, lane0):
    return acc[0:HEAD_DIM, lane0:lane0 + TQ] / acc[HEAD_DIM:HEAD_DIM + 1, lane0:lane0 + TQ]


def _overlap_band(ncmp_pad, n_slc):
    ratio = SLC_LEN // CMP_STRIDE
    band = {}
    for j in range(n_slc):
        for c in range(ncmp_pad - 1):
            ov = min(c * CMP_STRIDE + CMP_LEN, (j + 1) * SLC_LEN) - max(c * CMP_STRIDE, j * SLC_LEN)
            if ov > 0:
                k = c - ratio * j
                assert band.setdefault(k, ov / CMP_LEN) == ov / CMP_LEN and -ratio <= k
    assert ratio * (n_slc - 1) + max(band) < ncmp_pad
    return ratio, band


def _mixers_kernel(sink_ref,
                   qa_ref, qb_ref, qc_ref, ka_ref, kb_ref, ks_ref, kw_ref, va_ref, vb_ref, vs_ref, vw_ref,
                   kcmp_ref, vcmpt_ref, ga_ref, gb_ref, gc_ref, glt_ref,
                   lq1_ref, lk1_ref, lq2_ref, lk2_ref, sub_ref, kx_ref, qxa_ref, qxb_ref, qxc_ref,
                   y_ref,
                   qsta_ref, qstb_ref, qstc_ref, vauga_ref, vaugb_ref, vaugc_ref,
                   ea_ref, eb_ref, ec_ref, acca_ref, accb_ref, accc_ref, ma_ref, mb_ref, mc_ref,
                   msel_ref, st_ref, psum_ref, *, layer, lam_init):
    qi = pl.program_id(1)
    s_len = ka_ref.shape[0]
    n_slc = s_len // SLC_LEN
    ncmp = kcmp_ref.shape[0]
    blocks_per_chunk = TK // SLC_LEN
    cpg = C_HEADS // C_KV

    @pl.when(qi == 0)
    def _():
        _fill_value_rows(vauga_ref, va_ref, A_HEADS)
        _fill_value_rows(vaugb_ref, vb_ref, B_KV)
        _fill_value_rows(vaugc_ref.at[0], vs_ref, C_KV)
        _fill_value_rows(vaugc_ref.at[1], vw_ref, C_KV)
        _set_ext_rows(qsta_ref, qxa_ref)
        _set_ext_rows(qstb_ref, qxb_ref)
        _set_ext_rows(qstc_ref, qxc_ref)

    sub32 = lax.broadcasted_iota(jnp.int32, (LANES, TQ), 0) // A_QK
    for h in range(A_HEADS):
        blk = h * HEAD_DIM // LANES
        q = qa_ref[blk * LANES:(blk + 1) * LANES, :]
        zero = jnp.zeros_like(q)
        first = 2 * (h % (LANES // HEAD_DIM))
        qsta_ref[h, 0:LANES, :] = jnp.concatenate(
            [jnp.where(sub32 == first, q, zero), jnp.where(sub32 == first + 1, q, zero)], axis=1)
    _stack_gqa_q(qb_ref, qstb_ref, B_HEADS // B_KV)
    _stack_gqa_q(qc_ref, qstc_ref, cpg)
    acca_ref[...] = jnp.zeros(acca_ref.shape, F32)
    accc_ref[...] = jnp.zeros(accc_ref.shape, F32)
    t0 = qi * TQ
    den_rows = lax.broadcasted_iota(jnp.int32, (V_ROWS, 2 * TQ), 0) >= HEAD_DIM
    for g in range(B_KV):
        accb_ref[g] = jnp.where(den_rows, 1.0, 0.0)
    t_abs = (t0 + lax.broadcasted_iota(jnp.int32, (1, TQ), 1)).astype(F32)
    for c in range(B_HEADS):
        mb_ref[c:c + 1, :] = sink_ref[layer, c] * LOG2E + _SLOPE_B[c] * t_abs
    ma_ref[...] = jnp.full(ma_ref.shape, NEG, F32)
    mc_ref[...] = jnp.full(mc_ref.shape, NEG, F32)
    row_i = lax.broadcasted_iota(jnp.int32, (TK, TQ), 0)
    row_f = row_i.astype(F32)
    kmq = (row_i - lax.broadcasted_iota(jnp.int32, (TK, TQ), 1)).astype(F32)

    def start(ch):
        return pl.multiple_of(ch * TK, TK)

    masks = {}

    def cached(tag, build):
        if tag not in masks:
            masks[tag] = build()
        return masks[tag]

    def causal_mask(tag, s0):
        return cached(tag, lambda: jnp.where(kmq + (s0 - t0).astype(F32) <= 0.0, 0.0, NEG))

    def make_item(keys, weights, rows, madd, m_ref, e_ref, c0, acc_ref, idx, values):
        def scores():
            return _dot(keys(), weights())

        def softmax(st):
            return _flash_pair(st[0:rows], c0, madd(), m_ref, e_ref)

        def accumulate(alphas):
            _accumulate(acc_ref, idx, values(), e_ref, c0, alphas)

        return scores, softmax, accumulate

    def item_a(tag, ch, diagonal, h):
        blk = h * HEAD_DIM // LANES
        return make_item(
            lambda: _ext_keys(ka_ref[pl.ds(start(ch), TK), blk * LANES:(blk + 1) * LANES], kx_ref, start(ch), False),
            lambda: qsta_ref[h], TK,
            lambda: causal_mask(tag, start(ch)) if diagonal else None,
            ma_ref, ea_ref, 2 * h, acca_ref, h, lambda: vauga_ref[h, :, pl.ds(start(ch), TK)])

    def item_s(tag, ch, diagonal, p):
        g = 2 * p // cpg

        def weights():
            rows = msel_ref[g, ch]
            qstc_ref[p, LANES + BF16_ROWS:LANES + EXT_ROWS, :] = jnp.concatenate([rows, rows], axis=1)
            return qstc_ref[p]

        return make_item(
            lambda: _ext_keys(ks_ref[pl.ds(start(ch), TK), :], kx_ref, start(ch), True), weights, TK,
            lambda: causal_mask(tag, start(ch)) if diagonal else None,
            mc_ref.at[0], ec_ref, 2 * p, accc_ref.at[0], p, lambda: vaugc_ref[0, g, :, pl.ds(start(ch), TK)])

    def item_w(r, p):
        g = 2 * p // cpg
        n_full = C_WINDOW // TK
        rows = TK if r < n_full else TQ
        nominal = t0 - C_WINDOW + r * TK
        s0 = pl.multiple_of(jnp.maximum(nominal, 0), TQ)

        def madd():
            if r == n_full:
                return cached('own', lambda: jnp.where(kmq[0:rows] <= 0.0, 0.0, NEG))

            def build():
                ok = row_f < (nominal + TK - s0).astype(F32)
                if r == 0:
                    ok = jnp.logical_and(ok, kmq + (s0 - t0).astype(F32) > -float(C_WINDOW))
                return jnp.where(ok, 0.0, NEG)
            return cached(('win', r), build)

        return make_item(
            lambda: _ext_keys(kw_ref[pl.ds(s0, rows), :], kx_ref, s0, False), lambda: qstc_ref[p], rows, madd,
            mc_ref.at[1], ec_ref, 2 * p, accc_ref.at[1], p, lambda: vaugc_ref[1, g, :, pl.ds(s0, rows)])

    def item_b(g):
        s0 = pl.multiple_of(jnp.maximum(t0 - B_WINDOW, 0), LANES)

        def madd():
            nd = kmq + (s0 - t0).astype(F32)
            return jnp.where(jnp.logical_and(nd <= 0.0, nd > -float(B_WINDOW)), 0.0, NEG)

        return make_item(
            lambda: _ext_keys(kb_ref[pl.ds(s0, TK), :], kx_ref, s0, False), lambda: qstb_ref[g], TK, madd,
            mb_ref, eb_ref, 2 * g, accb_ref, g, lambda: vaugb_ref[g, :, pl.ds(s0, TK)])

    def causal_items(tag, ch, diagonal):
        return ([item_a(tag, ch, diagonal, h) for h in range(A_HEADS)]
                + [item_s(tag, ch, diagonal, p) for p in range(C_HEADS // 2)])

    last = (t0 + TQ - 1) // TK
    first = causal_items('first', 0, False)[:AHEAD]
    for j, item in enumerate(first[:A_HEADS]):
        st_ref[j] = item[0]()

    p_cmp = []
    tpos = t0 + lax.broadcasted_iota(jnp.int32, (ncmp, TQ), 1)
    cend = lax.broadcasted_iota(jnp.int32, (ncmp, TQ), 0) * CMP_STRIDE + (CMP_LEN - 1)
    dist_c = (tpos - cend).astype(F32)
    valid_c = dist_c >= 0.0
    for p in range(C_HEADS // 2):
        st = _dot(kcmp_ref[...], qstc_ref[p, 0:LANES, :])
        for j in range(2):
            c = 2 * p + j
            sc = jnp.where(valid_c, st[:, j * TQ:(j + 1) * TQ] - _SLOPE_C[c] * dist_c, NEG)
            mx = jnp.max(sc, axis=0, keepdims=True)
            e = jnp.where(valid_c, jnp.exp2(sc - mx), 0.0)
            den = jnp.sum(e, axis=0, keepdims=True)
            p_cmp.append(e / jnp.where(den > 0.0, den, 1.0))
    o_cmp = [_dot(vcmpt_ref[g * HEAD_DIM:(g + 1) * HEAD_DIM, :],
                  jnp.concatenate(p_cmp[g * cpg:(g + 1) * cpg], axis=1).astype(BF16))
             for g in range(C_KV)]

    jidx = lax.broadcasted_iota(jnp.int32, (n_slc, TQ), 0)
    tl = t0 + lax.broadcasted_iota(jnp.int32, (n_slc, TQ), 1)
    cur = tl // SLC_LEN
    causal_b = jidx * SLC_LEN <= tl
    forced = jnp.logical_or(jidx == 0, jnp.logical_or(jidx == cur, jidx == cur - 1))
    ratio, band = _overlap_band(ncmp, n_slc)
    for g in range(C_KV):
        psum = p_cmp[g * cpg]
        for i in range(1, cpg):
            psum = psum + p_cmp[g * cpg + i]
        psum_ref[g] = psum
        imp = jnp.zeros((n_slc, TQ), F32)
        for k, coef in band.items():
            if k >= 0:
                rows = psum_ref[g, pl.ds(k, n_slc, stride=ratio), :]
            else:
                rows = pltpu.roll(psum_ref[g, pl.ds(ratio + k, n_slc, stride=ratio), :], 1, 0)
                rows = jnp.where(jidx == 0, 0.0, rows)
            imp = imp + coef * rows
        score = jnp.where(causal_b, jnp.where(forced, FORCE, imp), NEG)
        rank = jnp.zeros((n_slc, TQ), F32)
        for jp in range(n_slc):
            row = score[jp:jp + 1, :]
            tie = jnp.where(jidx > jp, 1.0, 0.0)
            rank = rank + jnp.where(row > score, 1.0, jnp.where(row == score, tie, 0.0))
        msel = jnp.where(rank < float(SLC_TOPN), 0.0, NEG)
        pad = jnp.zeros((BF16_ROWS - blocks_per_chunk, TQ), F32)
        for ch in range(n_slc // blocks_per_chunk):
            rows = msel[ch * blocks_per_chunk:(ch + 1) * blocks_per_chunk, :]
            msel_ref[g, ch] = jnp.concatenate([rows, pad], axis=0).astype(BF16)

    def two_chunks(i, carry):
        masks.clear()
        _pipelined(causal_items(0, 2 * i, False) + causal_items(1, 2 * i + 1, False), st_ref,
                   causal_items(2, 2 * i + 2, False)[:AHEAD])
        masks.clear()
        return carry

    lax.fori_loop(0, last // 2, two_chunks, 0)

    def tail_items():
        return ([item_w(r, p) for r in range(C_WINDOW // TK + 1) for p in range(C_HEADS // 2)]
                + [item_b(g) for g in range(B_KV)])

    @pl.when(last % 2 == 1)
    def _():
        masks.clear()
        _pipelined(causal_items(0, last - 1, False) + causal_items(1, last, True) + tail_items(), st_ref)
        masks.clear()

    @pl.when(last % 2 == 0)
    def _():
        masks.clear()
        _pipelined(causal_items(1, last, True) + tail_items(), st_ref)
        masks.clear()

    def emit(col0, outs, g_ref):
        for i in range(len(outs) // 2):
            o = jnp.transpose(jnp.concatenate(outs[2 * i:2 * i + 2], axis=0))
            gate = g_ref[:, i * LANES:(i + 1) * LANES]
            y_ref[:, col0 + i * LANES:col0 + (i + 1) * LANES] = (o * gate).astype(BF16)

    lam = (jnp.exp(jnp.sum(lq1_ref[...] * lk1_ref[...], axis=1, keepdims=True))
           - jnp.exp(jnp.sum(lq2_ref[...] * lk2_ref[...], axis=1, keepdims=True)) + lam_init)
    outs = []
    for h in range(A_HEADS):
        acc = acca_ref[h]
        d = _normalized(acc, 0) - lam * _normalized(acc, TQ)
        ms = jnp.mean(d * d, axis=0, keepdims=True)
        outs.append(((d * lax.rsqrt(ms + RMS_EPS)) * sub_ref[...]) * (1.0 - lam_init))
    emit(0, outs, ga_ref)

    outs = []
    for c in range(B_HEADS):
        outs.append(_normalized(accb_ref[c // 2], (c % 2) * TQ))
    emit(A_W, outs, gb_ref)

    sig = glt_ref[...]
    outs = []
    for c in range(C_HEADS):
        g, i = c // cpg, c % cpg
        gate = lambda br: sig[br * C_HEADS + c:br * C_HEADS + c + 1, :]
        lane0 = (c % 2) * TQ
        outs.append(gate(0) * o_cmp[g][:, i * TQ:(i + 1) * TQ]
                    + gate(1) * _normalized(accc_ref[0, c // 2], lane0)
                    + gate(2) * _normalized(accc_ref[1, c // 2], lane0))
    emit(A_W + B_W, outs, gc_ref)


def _mixers(l, sinks, hk3, hg3, ht, gt, kcmp, vcmpt, lq1, lk1, lq2, lk2, sub, lam_init):
    b, s, _ = hk3.shape
    assert B_WINDOW + TQ <= TK and B_HEADS <= SUBLANES and C_WINDOW % TK == 0 and TQ % BF16_ROWS == 0
    assert AHEAD <= A_HEADS
    n_slc = s // SLC_LEN
    ncmp = kcmp.shape[1]
    nq = s // TQ
    kx = _key_ext(s)
    qxa = _query_ext([_SLOPE_A[c // 2] for c in range(2 * A_HEADS)])
    qxb = _query_ext(_SLOPE_B)
    qxc = _query_ext(_SLOPE_C)
    const = lambda a: pl.BlockSpec(a.shape, lambda bi, qi: (0,) * a.ndim)
    qt = lambda off, rows: pl.BlockSpec((rows, TQ), lambda bi, qi: (off // rows, bi * nq + qi))
    vt = lambda off, rows: pl.BlockSpec((rows, s), lambda bi, qi: (off // rows, bi))
    krow = lambda off, cols: pl.BlockSpec((None, s, cols), lambda bi, qi: (bi, 0, off // cols))
    gate = lambda off, cols: pl.BlockSpec((None, TQ, cols), lambda bi, qi: (bi, qi, off // cols))
    per_batch = lambda a: pl.BlockSpec((None,) + a.shape[1:], lambda bi, qi: (bi, 0, 0))
    return pl.pallas_call(
        functools.partial(_mixers_kernel, layer=l, lam_init=lam_init),
        grid=(b, nq),
        in_specs=[pl.BlockSpec(memory_space=pltpu.SMEM),
                  qt(T_QA, A_W), qt(T_QB, B_W), qt(T_QC, C_W),
                  krow(K_KA, A_W), krow(K_KB, KV_B), krow(K_KS, KV_C), krow(K_KW, KV_C),
                  vt(T_VA, A_W), vt(T_VB, KV_B), vt(T_VS, KV_C), vt(T_VW, KV_C),
                  per_batch(kcmp), per_batch(vcmpt),
                  gate(G_GA, A_W), gate(G_GB, B_W), gate(G_GC, C_W),
                  pl.BlockSpec((NGT, TQ), lambda bi, qi: (0, bi * nq + qi)),
                  _layer(lq1, l), _layer(lk1, l), _layer(lq2, l), _layer(lk2, l), _layer(sub, l),
                  const(kx), const(qxa), const(qxb), const(qxc)],
        out_specs=pl.BlockSpec((None, TQ, D_MIX), lambda bi, qi: (bi, qi, 0)),
        out_shape=jax.ShapeDtypeStruct((b, s, D_MIX), BF16),
        scratch_shapes=[pltpu.VMEM((A_HEADS, 2 * LANES, 2 * TQ), BF16),
                        pltpu.VMEM((B_HEADS // 2, 2 * LANES, 2 * TQ), BF16),
                        pltpu.VMEM((C_HEADS // 2, 2 * LANES, 2 * TQ), BF16),
                        pltpu.VMEM((A_HEADS, V_ROWS, s), BF16),
                        pltpu.VMEM((B_KV, V_ROWS, s), BF16),
                        pltpu.VMEM((2, C_KV, V_ROWS, s), BF16),
                        pltpu.VMEM((TK, 2 * A_HEADS * TQ), BF16),
                        pltpu.VMEM((TK, B_HEADS * TQ), BF16),
                        pltpu.VMEM((TK, C_HEADS * TQ), BF16),
                        pltpu.VMEM((A_HEADS, V_ROWS, 2 * TQ), F32),
                        pltpu.VMEM((B_HEADS // 2, V_ROWS, 2 * TQ), F32),
                        pltpu.VMEM((2, C_HEADS // 2, V_ROWS, 2 * TQ), F32),
                        pltpu.VMEM((2 * A_HEADS, TQ), F32),
                        pltpu.VMEM((SUBLANES, TQ), F32),
                        pltpu.VMEM((2, C_HEADS, TQ), F32),
                        pltpu.VMEM((C_KV, s // TK, BF16_ROWS, TQ), BF16),
                        pltpu.VMEM((AHEAD, TK, 2 * TQ), F32),
                        pltpu.VMEM((C_KV, ncmp, TQ), F32)],
        compiler_params=_params("arbitrary", "arbitrary"),
        name="mixers",
    )(sinks, ht, ht, ht, hk3, hk3, hk3, hk3, ht, ht, ht, ht, kcmp, vcmpt, hg3, hg3, hg3, gt,
      lq1, lk1, lq2, lk2, sub, kx, qxa, qxb, qxc)


def kernel(x, norm_w, w_in, w_out, diff_lq1, diff_lk1, diff_lq2, diff_lk2, diff_subln, sinks,
           cmp_pe_k, cmp_pe_v, cmp_wk1, cmp_wk2, cmp_wv1, cmp_wv2, final_norm):
    b, s, d = x.shape
    depth = w_in.shape[0]
    assert s % TK == 0 and s // TK >= 3 and (b * s) % TM == 0 and d % LANES == 0

    w_rm, w_t = _prep_w_in(w_in)
    w_out_b = w_out.astype(BF16)
    pe2 = lambda pe: jnp.concatenate([pe, pe], axis=-1)
    w1 = lambda w: jnp.concatenate([w.reshape(depth, CMP_LEN, HEAD_DIM, CMP_HID)] * 2, axis=2).astype(BF16)
    zero2 = jnp.zeros((depth, CMP_HID, HEAD_DIM), F32)
    w2 = lambda w: jnp.stack([jnp.concatenate([w, zero2], axis=-1),
                              jnp.concatenate([zero2, w], axis=-1)], axis=1)
    pek, pev = pe2(cmp_pe_k), pe2(cmp_pe_v)
    wk1, wv1 = w1(cmp_wk1), w1(cmp_wv1)
    wk2 = w2(cmp_wk2).astype(BF16)
    wv2t = jnp.swapaxes(w2(cmp_wv2), 2, 3).astype(BF16)
    sub = jnp.broadcast_to(diff_subln[:, :, None], (depth, HEAD_DIM, TQ))
    fw = final_norm[None, :]
    nw = norm_w[:, None, :]
    lam_rows = [a[:, None, :] for a in (diff_lq1, diff_lk1, diff_lq2, diff_lk2)]
    x2 = x.reshape(b * s, d)
    y = None
    for l in range(depth):
        if y is None:
            hk, hg, ht, gt = _in_proj(l, x2, nw, w_rm, w_t)
        else:
            x2, hk, hg, ht, gt = _in_proj(l, x2, nw, w_rm, w_t, y, w_out_b)
        hk3 = hk.reshape(b, s, NK)
        hg3 = hg.reshape(b, s, NG)
        kcmp, vcmpt = _compress(l, hg3, pek, pev, wk1, wk2, wv1, wv2t)
        lam_init = 0.8 - 0.6 * math.exp(-0.3 * l)
        y = _mixers(l, sinks, hk3, hg3, ht, gt, kcmp, vcmpt, *lam_rows, sub, lam_init).reshape(b * s, D_MIX)
    return _out_proj(depth - 1, y, w_out_b, x2, fw).reshape(b, s, d)
```
